```python
import jax, jax.numpy as jnp
from jax import lax
import numpy as np

D_MODEL = 4096
BATCH = 4
SEQ = 2048
DEPTH = 1
DEC_BATCH = 128
DEC_SEQ = 1
PAST_LEN = 16384
PAGE_SIZE = 128

W_A = D_MODEL // 2
W_B = D_MODEL // 2
CONV_A = 3
CONV_B = 31
EPS = 1e-6
SPLITS = [W_A, W_A, W_A, W_A, W_B, W_B, W_B, D_MODEL, D_MODEL]
P_IN = sum(SPLITS)

kernel_name = "hybrid_gated_conv_conformer_decode_step"


def _rmsnorm(x, g):
    xf = x.astype(jnp.float32)
    r = xf * lax.rsqrt(jnp.mean(xf * xf, axis=-1, keepdims=True) + EPS)
    return (r * g.astype(jnp.float32)).astype(x.dtype)


def _layernorm(x, g, b):
    xf = x.astype(jnp.float32)
    mu = jnp.mean(xf, axis=-1, keepdims=True)
    var = jnp.mean(jnp.square(xf - mu), axis=-1, keepdims=True)
    r = (xf - mu) * lax.rsqrt(var + EPS)
    return (r * g.astype(jnp.float32) + b.astype(jnp.float32)).astype(x.dtype)


def _causal_dwconv(u, prev, w, b):
    k, c = w.shape
    up = jnp.concatenate([prev.astype(u.dtype), u], axis=1)
    out = lax.conv_general_dilated(
        up, w[:, None, :].astype(u.dtype), window_strides=(1,), padding='VALID',
        dimension_numbers=('NWC', 'WIO', 'NWC'), feature_group_count=c)
    return out + b.astype(u.dtype), up[:, -(k - 1):, :]


def _layer(x, prev_a, prev_b, norm_gain, w_in, conv_a_w, conv_a_b, w_out_a,
           conv_b_w, conv_b_b, ln_b_gain, ln_b_bias, w_out_b, w_o):
    h = _rmsnorm(x, norm_gain)
    proj = jnp.einsum('btd,dp->btp', h, w_in)
    idx = list(np.cumsum(SPLITS)[:-1])
    b_a, c_a, v_a, z_a, u_a, u_g, z_b, g_a, g_b = jnp.split(proj, idx, axis=-1)
    ya, new_a = _causal_dwconv(c_a * v_a, prev_a, conv_a_w, conv_a_b)
    ya = b_a * ya * jax.nn.silu(z_a)
    oa = jnp.einsum('btw,wd->btd', ya, w_out_a)
    u = u_a * jax.nn.sigmoid(u_g)
    yb, new_b = _causal_dwconv(u, prev_b, conv_b_w, conv_b_b)
    yb = jax.nn.silu(_layernorm(yb, ln_b_gain, ln_b_bias)) * jax.nn.silu(z_b)
    ob = jnp.einsum('btw,wd->btd', yb, w_out_b)
    mix = jax.nn.sigmoid(g_a) * oa + jax.nn.sigmoid(g_b) * ob
    return x + jnp.einsum('btd,de->bte', mix, w_o), new_a, new_b


def setup_inputs(seed: int = 0) -> dict:
    key = jax.random.key(seed)
    ks = jax.random.split(key, 16)
    f = jnp.float32
    n = lambda k, s, sc: jax.random.normal(k, s, f) * sc
    return {
        "x_prompt": n(ks[0], (BATCH, SEQ, D_MODEL), 1.0),
        "x_sample": n(ks[1], (DEC_BATCH, DEC_SEQ, D_MODEL), 1.0),
        "state_conv_a": n(ks[2], (DEPTH, DEC_BATCH, CONV_A - 1, W_A), 1.0),
        "state_conv_b": n(ks[3], (DEPTH, DEC_BATCH, CONV_B - 1, W_B), 0.5),
        "norm_gain": 1.0 + n(ks[4], (DEPTH, D_MODEL), 0.01),
        "w_in": n(ks[5], (DEPTH, D_MODEL, P_IN), D_MODEL ** -0.5),
        "conv_a_w": n(ks[6], (DEPTH, CONV_A, W_A), CONV_A ** -0.5),
        "conv_a_b": n(ks[7], (DEPTH, W_A), 0.01),
        "w_out_a": n(ks[8], (DEPTH, W_A, D_MODEL), W_A ** -0.5),
        "conv_b_w": n(ks[9], (DEPTH, CONV_B, W_B), CONV_B ** -0.5),
        "conv_b_b": n(ks[10], (DEPTH, W_B), 0.01),
        "ln_b_gain": 1.0 + n(ks[11], (DEPTH, W_B), 0.01),
        "ln_b_bias": n(ks[12], (DEPTH, W_B), 0.01),
        "w_out_b": n(ks[13], (DEPTH, W_B, D_MODEL), W_B ** -0.5),
        "w_o": n(ks[14], (DEPTH, D_MODEL, D_MODEL), D_MODEL ** -0.5),
        "final_norm_gain": 1.0 + n(ks[15], (D_MODEL,), 0.01),
    }


def reference(x_prompt, x_sample, state_conv_a, state_conv_b, norm_gain, w_in,
              conv_a_w, conv_a_b, w_out_a, conv_b_w, conv_b_b, ln_b_gain,
              ln_b_bias, w_out_b, w_o, final_norm_gain):
    hp, hs = x_prompt, x_sample
    pa_list, pb_list, sa_list, sb_list = [], [], [], []
    for l in range(DEPTH):
        params = (norm_gain[l], w_in[l], conv_a_w[l], conv_a_b[l], w_out_a[l],
                  conv_b_w[l], conv_b_b[l], ln_b_gain[l], ln_b_bias[l],
                  w_out_b[l], w_o[l])
        zero_a = jnp.zeros((hp.shape[0], CONV_A - 1, W_A), hp.dtype)
        zero_b = jnp.zeros((hp.shape[0], CONV_B - 1, W_B), hp.dtype)
        hp, pa, pb = _layer(hp, zero_a, zero_b, *params)
        hs, sa, sb = _layer(hs, state_conv_a[l], state_conv_b[l], *params)
        pa_list.append(pa); pb_list.append(pb)
        sa_list.append(sa); sb_list.append(sb)
    y_prompt = _rmsnorm(hp, final_norm_gain)
    y_sample = _rmsnorm(hs, final_norm_gain)
    new_conv_a_prompt = jnp.stack(pa_list, axis=0)
    new_conv_b_prompt = jnp.stack(pb_list, axis=0)
    new_conv_a_sample = jnp.stack(sa_list, axis=0)
    new_conv_b_sample = jnp.stack(sb_list, axis=0)
    return (y_prompt, y_sample, new_conv_a_prompt, new_conv_b_prompt, new_conv_a_sample, new_conv_b_sample)
```

```python
import functools

import jax
import jax.numpy as jnp
from jax import lax
from jax.experimental import pallas as pl
from jax.experimental.pallas import tpu as pltpu

EPS = 1e-6
CONV_A = 3
CONV_B = 31
SUBLANES = 8
HIST_A = SUBLANES
HIST_B = 4 * SUBLANES
VMEM_LIMIT = 56 * 1024 * 1024

F32 = jnp.float32
BF16 = jnp.bfloat16


def _params(n_axes):
    return pltpu.CompilerParams(
        dimension_semantics=("arbitrary",) * n_axes, vmem_limit_bytes=VMEM_LIMIT)


def _dot(a, w_ref):
    return jnp.dot(a, w_ref[...], preferred_element_type=F32)


def _silu(x):
    return x * jax.nn.sigmoid(x)


def _rmsnorm_kernel(x_ref, g_ref, h_ref):
    x = x_ref[...]
    ms = jnp.mean(x * x, axis=-1, keepdims=True)
    h_ref[...] = (x * lax.rsqrt(ms + EPS) * g_ref[...]).astype(h_ref.dtype)


def _rmsnorm_cast(x, gain, tm):
    m, d = x.shape
    return pl.pallas_call(
        _rmsnorm_kernel,
        grid=(m // tm,),
        in_specs=[pl.BlockSpec((tm, d), lambda i: (i, 0)),
                  pl.BlockSpec((1, d), lambda i: (0, 0))],
        out_specs=pl.BlockSpec((tm, d), lambda i: (i, 0)),
        out_shape=jax.ShapeDtypeStruct((m, d), BF16),
        compiler_params=_params(1),
    )(x, gain.reshape(1, d))


def _branch_a_prompt_kernel(h_ref, wb_ref, wc_ref, wv_ref, wz_ref, cw_ref, cb_ref,
                            ya_ref, cvlast_ref, buf_ref, carry_ref, *, tiles_per_seq):
    m, j = pl.program_id(0), pl.program_id(1)
    tm = h_ref.shape[0]
    h = h_ref[...]
    cv = _dot(h, wc_ref) * _dot(h, wv_ref)

    @pl.when(m % tiles_per_seq == 0)
    def _():
        buf_ref[0:HIST_A, :] = jnp.zeros((HIST_A, cv.shape[1]), F32)

    @pl.when(m % tiles_per_seq != 0)
    def _():
        buf_ref[0:HIST_A, :] = carry_ref[j]

    buf_ref[HIST_A:, :] = cv
    last = cv[tm - HIST_A:, :]
    carry_ref[j] = last
    cvlast_ref[0] = last
    conv = (cw_ref[0:1, :] * buf_ref[HIST_A - 2:HIST_A - 2 + tm, :]
            + cw_ref[1:2, :] * buf_ref[HIST_A - 1:HIST_A - 1 + tm, :]
            + cw_ref[2:3, :] * cv + cb_ref[...])
    ya = _dot(h, wb_ref) * conv * _silu(_dot(h, wz_ref))
    ya_ref[...] = ya.astype(ya_ref.dtype)


def _branch_a_sample_kernel(h_ref, wb_ref, wc_ref, wv_ref, wz_ref, cw_ref, cb_ref, s0_ref, s1_ref,
                            ya_ref, cv_ref):
    h = h_ref[...]
    cv = _dot(h, wc_ref) * _dot(h, wv_ref)
    cv_ref[...] = cv
    conv = (cw_ref[0:1, :] * s0_ref[...] + cw_ref[1:2, :] * s1_ref[...]
            + cw_ref[2:3, :] * cv + cb_ref[...])
    ya = _dot(h, wb_ref) * conv * _silu(_dot(h, wz_ref))
    ya_ref[...] = ya.astype(ya_ref.dtype)


def _branch_a(h, w_in, conv_w, conv_b, w_a, *, tm, cw, seq_len=None, state=None):
    m, d = h.shape
    n_chunks = w_a // cw
    col = lambda k: (lambda i, j: (0, k * n_chunks + j))
    in_specs = [pl.BlockSpec((tm, d), lambda i, j: (i, 0))]
    in_specs += [pl.BlockSpec((d, cw), col(k)) for k in (0, 1, 2, 3)]
    in_specs += [pl.BlockSpec((CONV_A, cw), lambda i, j: (0, j)),
                 pl.BlockSpec((1, cw), lambda i, j: (0, j))]
    args = [h, w_in, w_in, w_in, w_in, conv_w, conv_b.reshape(1, w_a)]
    ya_spec = pl.BlockSpec((tm, cw), lambda i, j: (i, j))
    ya_shape = jax.ShapeDtypeStruct((m, w_a), BF16)
    if state is None:
        tiles_per_seq = seq_len // tm
        kernel = functools.partial(_branch_a_prompt_kernel, tiles_per_seq=tiles_per_seq)
        out_specs = [ya_spec, pl.BlockSpec((1, HIST_A, cw), lambda i, j: (i, 0, j))]
        out_shape = [ya_shape, jax.ShapeDtypeStruct((m // tm, HIST_A, w_a), F32)]
        scratch = [pltpu.VMEM((tm + HIST_A, cw), F32), pltpu.VMEM((n_chunks, HIST_A, cw), F32)]
    else:
        kernel = _branch_a_sample_kernel
        in_specs += [pl.BlockSpec((tm, cw), lambda i, j: (i, j))] * 2
        args += [state[:, 0, :], state[:, 1, :]]
        out_specs = [ya_spec, pl.BlockSpec((tm, cw), lambda i, j: (i, j))]
        out_shape = [ya_shape, jax.ShapeDtypeStruct((m, w_a), F32)]
        scratch = []
    ya, tail = pl.pallas_call(
        kernel, grid=(m // tm, n_chunks), in_specs=in_specs, out_specs=out_specs,
        out_shape=out_shape, scratch_shapes=scratch, compiler_params=_params(2),
    )(*args)
    if state is None:
        tail = tail[tiles_per_seq - 1::tiles_per_seq]
    return ya, tail


LN_ROWS = 16


def _layernorm_gate(conv_ref, sz_ref, lng_ref, lnb_ref, yb_ref):
    tm, w = conv_ref.shape
    g = lng_ref[...]
    b = lnb_ref[...]

    def body(i, _):
        r0 = pl.multiple_of(i * LN_ROWS, LN_ROWS)
        x = conv_ref[pl.ds(r0, LN_ROWS), :]
        mu = jnp.mean(x, axis=-1, keepdims=True)
        xc = x - mu
        var = jnp.mean(xc * xc, axis=-1, keepdims=True)
        r = xc * lax.rsqrt(var + EPS) * g + b
        yb_ref[pl.ds(r0, LN_ROWS), :] = (_silu(r) * sz_ref[pl.ds(r0, LN_ROWS), :]).astype(yb_ref.dtype)
        return 0

    lax.fori_loop(0, tm // LN_ROWS, body, 0)


CONV_ROWS = 64


def _branch_b_prompt_kernel(h_ref, wua_ref, wug_ref, wzb_ref, cw_ref, cb_ref, lng_ref, lnb_ref,
                            yb_ref, ulast_ref, ubuf_ref, carry_ref, conv_ref, sz_ref,
                            *, tiles_per_seq):
    m, j = pl.program_id(0), pl.program_id(1)
    tm = h_ref.shape[0]
    cw = cw_ref.shape[1]
    h = h_ref[...]
    u = _dot(h, wua_ref) * jax.nn.sigmoid(_dot(h, wug_ref))
    lanes = pl.ds(pl.multiple_of(j * cw, cw), cw)
    sz_ref[:, lanes] = _silu(_dot(h, wzb_ref))

    @pl.when(m % tiles_per_seq == 0)
    def _():
        ubuf_ref[0:HIST_B, :] = jnp.zeros((HIST_B, cw), F32)

    @pl.when(m % tiles_per_seq != 0)
    def _():
        ubuf_ref[0:HIST_B, :] = carry_ref[j]

    ubuf_ref[HIST_B:, :] = u
    last = u[tm - HIST_B:, :]
    carry_ref[j] = last
    ulast_ref[0] = last

    first = HIST_B - (CONV_B - 1)
    for r0 in range(0, tm, CONV_ROWS):
        acc = cb_ref[...] + cw_ref[0:1, :] * ubuf_ref[r0 + first:r0 + first + CONV_ROWS, :]
        for k in range(1, CONV_B):
            acc = acc + cw_ref[k:k + 1, :] * ubuf_ref[r0 + first + k:r0 + first + k + CONV_ROWS, :]
        conv_ref[r0:r0 + CONV_ROWS, lanes] = acc

    @pl.when(j == pl.num_programs(1) - 1)
    def _():
        _layernorm_gate(conv_ref, sz_ref, lng_ref, lnb_ref, yb_ref)


def _branch_b_sample_kernel(h_ref, wua_ref, wug_ref, wzb_ref, cw_ref, cb_ref, lng_ref, lnb_ref, st_ref,
                            yb_ref, u_ref, conv_ref, sz_ref):
    j = pl.program_id(1)
    cw = cw_ref.shape[1]
    h = h_ref[...]
    u = _dot(h, wua_ref) * jax.nn.sigmoid(_dot(h, wug_ref))
    u_ref[...] = u
    lanes = pl.ds(pl.multiple_of(j * cw, cw), cw)
    sz_ref[:, lanes] = _silu(_dot(h, wzb_ref))
    acc = cb_ref[...] + cw_ref[CONV_B - 1:CONV_B, :] * u
    for k in range(CONV_B - 1):
        acc = acc + cw_ref[k:k + 1, :] * st_ref[k]
    conv_ref[:, lanes] = acc

    @pl.when(j == pl.num_programs(1) - 1)
    def _():
        _layernorm_gate(conv_ref, sz_ref, lng_ref, lnb_ref, yb_ref)


def _branch_b(h, w_in, conv_w, conv_b, ln_g, ln_b, w_b, col0, *, tm, cw, seq_len=None, state=None):
    m, d = h.shape
    n_chunks = w_b // cw
    col = lambda k: (lambda i, j: (0, col0 // cw + k * n_chunks + j))
    in_specs = [pl.BlockSpec((tm, d), lambda i, j: (i, 0))]
    in_specs += [pl.BlockSpec((d, cw), col(k)) for k in (0, 1, 2)]
    in_specs += [pl.BlockSpec((CONV_B, cw), lambda i, j: (0, j)),
                 pl.BlockSpec((1, cw), lambda i, j: (0, j)),
                 pl.BlockSpec((1, w_b), lambda i, j: (0, 0)),
                 pl.BlockSpec((1, w_b), lambda i, j: (0, 0))]
    args = [h, w_in, w_in, w_in, conv_w, conv_b.reshape(1, w_b), ln_g.reshape(1, w_b), ln_b.reshape(1, w_b)]
    yb_spec = pl.BlockSpec((tm, w_b), lambda i, j: (i, 0))
    yb_shape = jax.ShapeDtypeStruct((m, w_b), BF16)
    common_scratch = [pltpu.VMEM((tm, w_b), F32), pltpu.VMEM((tm, w_b), F32)]
    if state is None:
        tiles_per_seq = seq_len // tm
        kernel = functools.partial(_branch_b_prompt_kernel, tiles_per_seq=tiles_per_seq)
        out_specs = [yb_spec, pl.BlockSpec((1, HIST_B, cw), lambda i, j: (i, 0, j))]
        out_shape = [yb_shape, jax.ShapeDtypeStruct((m // tm, HIST_B, w_b), F32)]
        scratch = [pltpu.VMEM((tm + HIST_B, cw), F32), pltpu.VMEM((n_chunks, HIST_B, cw), F32)]
    else:
        kernel = _branch_b_sample_kernel
        in_specs += [pl.BlockSpec((CONV_B - 1, tm, cw), lambda i, j: (0, i, j))]
        args += [jnp.transpose(state, (1, 0, 2))]
        out_specs = [yb_spec, pl.BlockSpec((tm, cw), lambda i, j: (i, j))]
        out_shape = [yb_shape, jax.ShapeDtypeStruct((m, w_b), F32)]
        scratch = []
    yb, tail = pl.pallas_call(
        kernel, grid=(m // tm, n_chunks), in_specs=in_specs, out_specs=out_specs,
        out_shape=out_shape, scratch_shapes=scratch + common_scratch, compiler_params=_params(2),
    )(*args)
    if state is None:
        tail = tail[tiles_per_seq - 1::tiles_per_seq]
    return yb, tail


def _merge_kernel(h_ref, ya_ref, yb_ref, wga_ref, wgb_ref, woa_ref, wob_ref, mix_ref):
    h = h_ref[...]
    mix = (jax.nn.sigmoid(_dot(h, wga_ref)) * _dot(ya_ref[...], woa_ref)
           + jax.nn.sigmoid(_dot(h, wgb_ref)) * _dot(yb_ref[...], wob_ref))
    mix_ref[...] = mix.astype(mix_ref.dtype)


def _merge(h, ya, yb, w_in, w_out_a, w_out_b, col0, *, tm, cn):
    m, d = h.shape
    w = ya.shape[1]
    n_chunks = d // cn
    gcol = lambda k: (lambda i, j: (0, col0 // cn + k * n_chunks + j))
    return pl.pallas_call(
        _merge_kernel,
        grid=(m // tm, n_chunks),
        in_specs=[pl.BlockSpec((tm, d), lambda i, j: (i, 0)),
                  pl.BlockSpec((tm, w), lambda i, j: (i, 0)),
                  pl.BlockSpec((tm, w), lambda i, j: (i, 0)),
                  pl.BlockSpec((d, cn), gcol(0)),
                  pl.BlockSpec((d, cn), gcol(1)),
                  pl.BlockSpec((w, cn), lambda i, j: (0, j)),
                  pl.BlockSpec((w, cn), lambda i, j: (0, j))],
        out_specs=pl.BlockSpec((tm, cn), lambda i, j: (i, j)),
        out_shape=jax.ShapeDtypeStruct((m, d), BF16),
        compiler_params=_params(2),
    )(h, ya, yb, w_in, w_in, w_out_a, w_out_b)


def _out_kernel(mix_ref, x_ref, wo_ref, g_ref, y_ref, ssq_ref):
    j = pl.program_id(1)
    cn = wo_ref.shape[1]
    o = x_ref[...] + _dot(mix_ref[...], wo_ref)
    y_ref[:, pl.ds(pl.multiple_of(j * cn, cn), cn)] = o
    part = jnp.sum(o * o, axis=-1, keepdims=True)

    @pl.when(j == 0)
    def _():
        ssq_ref[...] = part

    @pl.when(j != 0)
    def _():
        ssq_ref[...] += part

    @pl.when(j == pl.num_programs(1) - 1)
    def _():
        scale = lax.rsqrt(ssq_ref[...] / y_ref.shape[1] + EPS)
        y_ref[...] = y_ref[...] * scale * g_ref[...]


def _out_proj(mix, x, w_o, gain, *, tm, cn):
    m, d = x.shape
    return pl.pallas_call(
        _out_kernel,
        grid=(m // tm, d // cn),
        in_specs=[pl.BlockSpec((tm, d), lambda i, j: (i, 0)),
                  pl.BlockSpec((tm, cn), lambda i, j: (i, j)),
                  pl.BlockSpec((d, cn), lambda i, j: (0, j)),
                  pl.BlockSpec((1, d), lambda i, j: (0, 0))],
        out_specs=pl.BlockSpec((tm, d), lambda i, j: (i, 0)),
        out_shape=jax.ShapeDtypeStruct((m, d), F32),
        scratch_shapes=[pltpu.VMEM((tm, 1), F32)],
        compiler_params=_params(2),
    )(mix, x, w_o, gain.reshape(1, d))


def _tiles(m):
    if m >= 1024:
        return dict(norm=256, a=1024, b=512, merge=512, out=512)
    return dict(norm=m, a=m, b=m, merge=m, out=m)


def _layer(x, w, *, seq_len=None, state_a=None, state_b=None):
    m, d = x.shape
    w_a = w["conv_a_w"].shape[1]
    w_b = w["conv_b_w"].shape[1]
    t = _tiles(m)
    h = _rmsnorm_cast(x, w["norm_gain"], t["norm"])
    ya, a_tail = _branch_a(h, w["w_in"], w["conv_a_w"], w["conv_a_b"], w_a,
                           tm=t["a"], cw=256, seq_len=seq_len, state=state_a)
    yb, b_tail = _branch_b(h, w["w_in"], w["conv_b_w"], w["conv_b_b"], w["ln_b_gain"], w["ln_b_bias"],
                           w_b, 4 * w_a, tm=t["b"], cw=256, seq_len=seq_len, state=state_b)
    mix = _merge(h, ya, yb, w["w_in"], w["w_out_a"], w["w_out_b"], 4 * w_a + 3 * w_b, tm=t["merge"], cn=256)
    y = _out_proj(mix, x, w["w_o"], w["final_norm_gain"], tm=t["out"], cn=512)
    return y, a_tail, b_tail


def kernel(x_prompt, x_sample, state_conv_a, state_conv_b, norm_gain, w_in, conv_a_w, conv_a_b, w_out_a,
           conv_b_w, conv_b_b, ln_b_gain, ln_b_bias, w_out_b, w_o, final_norm_gain):
    depth = norm_gain.shape[0]
    assert depth == 1, "the output projection kernel applies the final norm: one layer only"
    batch, seq, d = x_prompt.shape
    dec_batch, dec_seq, _ = x_sample.shape
    assert dec_seq == 1
    w = dict(norm_gain=norm_gain[0], w_in=w_in[0].astype(BF16), conv_a_w=conv_a_w[0], conv_a_b=conv_a_b[0],
             w_out_a=w_out_a[0].astype(BF16), conv_b_w=conv_b_w[0], conv_b_b=conv_b_b[0],
             ln_b_gain=ln_b_gain[0], ln_b_bias=ln_b_bias[0], w_out_b=w_out_b[0].astype(BF16),
             w_o=w_o[0].astype(BF16), final_norm_gain=final_norm_gain)

    yp, a_tail, b_tail = _layer(x_prompt.reshape(batch * seq, d), w, seq_len=seq)
    new_a_prompt = a_tail[:, HIST_A - (CONV_A - 1):, :]
    new_b_prompt = b_tail[:, HIST_B - (CONV_B - 1):, :]

    sa, sb = state_conv_a[0], state_conv_b[0]
    ys, cv_s, u_s = _layer(x_sample.reshape(dec_batch, d), w, state_a=sa, state_b=sb)
    new_a_sample = jnp.concatenate([sa[:, 1:, :], cv_s[:, None, :]], axis=1)
    new_b_sample = jnp.concatenate([sb[:, 1:, :], u_s[:, None, :]], axis=1)

    return (yp.reshape(batch, seq, d), ys.reshape(dec_batch, dec_seq, d),
            new_a_prompt[None], new_b_prompt[None], new_a_sample[None], new_b_sample[None])
```

```python
import functools

import jax
import jax.numpy as jnp
from jax import lax
from jax.experimental import pallas as pl
from jax.experimental.pallas import tpu as pltpu

EPS = 1e-6
CONV_A = 3
CONV_B = 31
SUBLANES = 8
HIST_A = SUBLANES
HIST_B = 4 * SUBLANES
VMEM_LIMIT = 56 * 1024 * 1024

F32 = jnp.float32
BF16 = jnp.bfloat16


def _params(n_axes):
    return pltpu.CompilerParams(
        dimension_semantics=("arbitrary",) * n_axes, vmem_limit_bytes=VMEM_LIMIT)


def _dot(a, w_ref):
    return jnp.dot(a, w_ref[...], preferred_element_type=F32)


def _silu(x):
    return x * jax.nn.sigmoid(x)


def _rmsnorm_kernel(x_ref, g_ref, h_ref):
    x = x_ref[...]
    ms = jnp.mean(x * x, axis=-1, keepdims=True)
    h_ref[...] = (x * lax.rsqrt(ms + EPS) * g_ref[...]).astype(h_ref.dtype)


def _rmsnorm_cast(x, gain, tm):
    m, d = x.shape
    return pl.pallas_call(
        _rmsnorm_kernel,
        grid=(m // tm,),
        in_specs=[pl.BlockSpec((tm, d), lambda i: (i, 0)),
                  pl.BlockSpec((1, d), lambda i: (0, 0))],
        out_specs=pl.BlockSpec((tm, d), lambda i: (i, 0)),
        out_shape=jax.ShapeDtypeStruct((m, d), BF16),
        compiler_params=_params(1),
    )(x, gain.reshape(1, d))


def _branch_a_prompt_kernel(h_ref, wb_ref, wc_ref, wv_ref, wz_ref, cw_ref, cb_ref,
                            ya_ref, cvlast_ref, buf_ref, carry_ref, *, tiles_per_seq):
    m, j = pl.program_id(0), pl.program_id(1)
    tm = h_ref.shape[0]
    h = h_ref[...]
    cv = _dot(h, wc_ref) * _dot(h, wv_ref)

    @pl.when(m % tiles_per_seq == 0)
    def _():
        buf_ref[0:HIST_A, :] = jnp.zeros((HIST_A, cv.shape[1]), F32)

    @pl.when(m % tiles_per_seq != 0)
    def _():
        buf_ref[0:HIST_A, :] = carry_ref[j]

    buf_ref[HIST_A:, :] = cv
    last = cv[tm - HIST_A:, :]
    carry_ref[j] = last
    cvlast_ref[0] = last
    conv = (cw_ref[0:1, :] * buf_ref[HIST_A - 2:HIST_A - 2 + tm, :]
            + cw_ref[1:2, :] * buf_ref[HIST_A - 1:HIST_A - 1 + tm, :]
            + cw_ref[2:3, :] * cv + cb_ref[...])
    ya = _dot(h, wb_ref) * conv * _silu(_dot(h, wz_ref))
    ya_ref[...] = ya.astype(ya_ref.dtype)


def _branch_a_sample_kernel(h_ref, wb_ref, wc_ref, wv_ref, wz_ref, cw_ref, cb_ref, s0_ref, s1_ref,
                            ya_ref, cv_ref):
    h = h_ref[...]
    cv = _dot(h, wc_ref) * _dot(h, wv_ref)
    cv_ref[...] = cv
    conv = (cw_ref[0:1, :] * s0_ref[...] + cw_ref[1:2, :] * s1_ref[...]
            + cw_ref[2:3, :] * cv + cb_ref[...])
    ya = _dot(h, wb_ref) * conv * _silu(_dot(h, wz_ref))
    ya_ref[...] = ya.astype(ya_ref.dtype)


def _branch_a(h, w_in, conv_w, conv_b, w_a, *, tm, cw, seq_len=None, state=None):
    m, d = h.shape
    n_chunks = w_a // cw
    col = lambda k: (lambda i, j: (0, k * n_chunks + j))
    in_specs = [pl.BlockSpec((tm, d), lambda i, j: (i, 0))]
    in_specs += [pl.BlockSpec((d, cw), col(k)) for k in (0, 1, 2, 3)]
    in_specs += [pl.BlockSpec((CONV_A, cw), lambda i, j: (0, j)),
                 pl.BlockSpec((1, cw), lambda i, j: (0, j))]
    args = [h, w_in, w_in, w_in, w_in, conv_w, conv_b.reshape(1, w_a)]
    ya_spec = pl.BlockSpec((tm, cw), lambda i, j: (i, j))
    ya_shape = jax.ShapeDtypeStruct((m, w_a), BF16)
    if state is None:
        tiles_per_seq = seq_len // tm
        kernel = functools.partial(_branch_a_prompt_kernel, tiles_per_seq=tiles_per_seq)
        out_specs = [ya_spec, pl.BlockSpec((1, HIST_A, cw), lambda i, j: (i, 0, j))]
        out_shape = [ya_shape, jax.ShapeDtypeStruct((m // tm, HIST_A, w_a), F32)]
        scratch = [pltpu.VMEM((tm + HIST_A, cw), F32), pltpu.VMEM((n_chunks, HIST_A, cw), F32)]
    else:
        kernel = _branch_a_sample_kernel
        in_specs += [pl.BlockSpec((tm, cw), lambda i, j: (i, j))] * 2
        args += [state[:, 0, :], state[:, 1, :]]
        out_specs = [ya_spec, pl.BlockSpec((tm, cw), lambda i, j: (i, j))]
        out_shape = [ya_shape, jax.ShapeDtypeStruct((m, w_a), F32)]
        scratch = []
    ya, tail = pl.pallas_call(
        kernel, grid=(m // tm, n_chunks), in_specs=in_specs, out_specs=out_specs,
        out_shape=out_shape, scratch_shapes=scratch, compiler_params=_params(2),
    )(*args)
    if state is None:
        tail = tail[tiles_per_seq - 1::tiles_per_seq]
    return ya, tail


LN_ROWS = 16


def _layernorm_gate(conv_ref, sz_ref, lng_ref, lnb_ref, yb_ref):
    tm, w = conv_ref.shape
    g = lng_ref[...]
    b = lnb_ref[...]

    def body(i, _):
        r0 = pl.multiple_of(i * LN_ROWS, LN_ROWS)
        x = conv_ref[pl.ds(r0, LN_ROWS), :]
        mu = jnp.mean(x, axis=-1, keepdims=True)
        xc = x - mu
        var = jnp.mean(xc * xc, axis=-1, keepdims=True)
        r = xc * lax.rsqrt(var + EPS) * g + b
        yb_ref[pl.ds(r0, LN_ROWS), :] = (_silu(r) * sz_ref[pl.ds(r0, LN_ROWS), :]).astype(yb_ref.dtype)
        return 0

    lax.fori_loop(0, tm // LN_ROWS, body, 0)


CONV_ROWS = 64


MXU_K = 256
ANCHOR_TILE = (16, 128)


def _anchored_weights(w_ref, anchors, never):
    ar, ac = ANCHOR_TILE
    pieces = []
    for i in range(w_ref.shape[0] // MXU_K):
        r0 = i * MXU_K
        if i not in anchors:
            pieces.append(w_ref[r0:r0 + MXU_K, :])
            continue
        first = jnp.where(never, anchors[i].astype(w_ref.dtype), w_ref[r0:r0 + ar, 0:ac])
        top = jnp.concatenate([first, w_ref[r0:r0 + ar, ac:]], axis=1)
        pieces.append(jnp.concatenate([top, w_ref[r0 + ar:r0 + MXU_K, :]], axis=0))
    return jnp.concatenate(pieces, axis=0)


def _fold_to_anchor(x):
    ar, ac = ANCHOR_TILE
    tiles = [x[r:r + ar, c:c + ac] for r in range(0, x.shape[0], ar) for c in range(0, x.shape[1], ac)]
    out = tiles[0]
    for t in tiles[1:]:
        out = out + t
    return out


def _branch_b_prompt_kernel(h_ref, wua_ref, wug_ref, wzb_ref, cw_ref, cb_ref, lng_ref, lnb_ref,
                            yb_ref, ulast_ref, ua_ref, ug_ref, zb_ref, ush_ref, carry_ref, conv_ref, sz_ref,
                            *, tiles_per_seq, n_chunks, n_steps):
    s = pl.program_id(0)
    prev = jnp.maximum(s - 1, 0)
    mp, jp = prev // n_chunks, prev % n_chunks
    tm = h_ref.shape[0]
    cw = cw_ref.shape[1]
    n_rows = tm + HIST_B
    lanes_p = pl.ds(pl.multiple_of(jp * cw, cw), cw)

    @pl.when(s == 0)
    def _():
        for ref in (ua_ref, ug_ref, zb_ref, carry_ref):
            ref[...] = jnp.zeros(ref.shape, F32)

    u = ua_ref[...] * jax.nn.sigmoid(ug_ref[...])
    szv = _silu(zb_ref[...])
    sz_ref[:, lanes_p] = szv
    hist = jnp.where(mp % tiles_per_seq != 0, carry_ref[jp], 0.0)
    tail = u[tm - HIST_B:, :]
    carry_ref[jp] = tail
    ulast_ref[0] = tail
    staged = jnp.concatenate([hist, u], axis=0)
    ush_ref[0] = staged
    for r in range(1, SUBLANES):
        ush_ref[r] = pltpu.roll(staged, n_rows - r, axis=0)

    first = HIST_B - (CONV_B - 1)

    def conv_rows(r0):
        acc = cb_ref[...]
        for k in range(CONV_B):
            r, q = (first + k) % SUBLANES, (first + k) // SUBLANES
            acc = acc + cw_ref[k:k + 1, :] * ush_ref[r, r0 + q * SUBLANES:r0 + q * SUBLANES + CONV_ROWS, :]
        conv_ref[r0:r0 + CONV_ROWS, lanes_p] = acc
        return _fold_to_anchor(acc)

    deps = [conv_rows(r0) for r0 in range(0, tm, CONV_ROWS)]
    deps.append(_fold_to_anchor(szv[:, 0:ANCHOR_TILE[1]]))

    n_slabs = h_ref.shape[1] // MXU_K
    first_slab, spacing = 6, 5
    anchors = [{}, {}, {}]
    for i, dep in enumerate(deps):
        slab = min(first_slab + spacing * i, 3 * n_slabs - 1)
        anchors[slab // n_slabs][slab % n_slabs] = dep
    never = s < 0
    h = h_ref[...]
    ua_ref[...] = jnp.dot(h, _anchored_weights(wua_ref, anchors[0], never), preferred_element_type=F32)
    ug_ref[...] = jnp.dot(h, _anchored_weights(wug_ref, anchors[1], never), preferred_element_type=F32)
    zb_ref[...] = jnp.dot(h, _anchored_weights(wzb_ref, anchors[2], never), preferred_element_type=F32)

    @pl.when(jnp.logical_and(s > 0, jp == n_chunks - 1))
    def _():
        _layernorm_gate(conv_ref, sz_ref, lng_ref, lnb_ref, yb_ref)


def _branch_b_prompt(h, w_in, conv_w, conv_b, ln_g, ln_b, w_b, col0, *, tm, cw, seq_len):
    m, d = h.shape
    n_chunks = w_b // cw
    n_tiles = m // tm
    n_steps = n_tiles * n_chunks
    tiles_per_seq = seq_len // tm
    cur = lambda s: jnp.minimum(s, n_steps - 1)
    prev = lambda s: jnp.maximum(s - 1, 0)
    col = lambda k: (lambda s: (0, col0 // cw + k * n_chunks + cur(s) % n_chunks))
    in_specs = [pl.BlockSpec((tm, d), lambda s: (cur(s) // n_chunks, 0))]
    in_specs += [pl.BlockSpec((d, cw), col(k)) for k in (0, 1, 2)]
    in_specs += [pl.BlockSpec((CONV_B, cw), lambda s: (0, prev(s) % n_chunks)),
                 pl.BlockSpec((1, cw), lambda s: (0, prev(s) % n_chunks)),
                 pl.BlockSpec((1, w_b), lambda s: (0, 0)),
                 pl.BlockSpec((1, w_b), lambda s: (0, 0))]
    kernel = functools.partial(_branch_b_prompt_kernel, tiles_per_seq=tiles_per_seq,
                               n_chunks=n_chunks, n_steps=n_steps)
    yb, tail = pl.pallas_call(
        kernel, grid=(n_steps + 1,), in_specs=in_specs,
        out_specs=[pl.BlockSpec((tm, w_b), lambda s: (prev(s) // n_chunks, 0)),
                   pl.BlockSpec((1, HIST_B, cw), lambda s: (prev(s) // n_chunks, 0, prev(s) % n_chunks))],
        out_shape=[jax.ShapeDtypeStruct((m, w_b), BF16),
                   jax.ShapeDtypeStruct((n_tiles, HIST_B, w_b), F32)],
        scratch_shapes=[pltpu.VMEM((tm, cw), F32),
                        pltpu.VMEM((tm, cw), F32),
                        pltpu.VMEM((tm, cw), F32),
                        pltpu.VMEM((SUBLANES, tm + HIST_B, cw), F32),
                        pltpu.VMEM((n_chunks, HIST_B, cw), F32),
                        pltpu.VMEM((tm, w_b), F32),
                        pltpu.VMEM((tm, w_b), F32)],
        compiler_params=_params(1),
    )(h, w_in, w_in, w_in, conv_w, conv_b.reshape(1, w_b), ln_g.reshape(1, w_b), ln_b.reshape(1, w_b))
    return yb, tail[tiles_per_seq - 1::tiles_per_seq]


def _branch_b_sample_kernel(h_ref, wua_ref, wug_ref, wzb_ref, cw_ref, cb_ref, lng_ref, lnb_ref, st_ref,
                            yb_ref, u_ref, conv_ref, sz_ref):
    j = pl.program_id(1)
    cw = cw_ref.shape[1]
    h = h_ref[...]
    u = _dot(h, wua_ref) * jax.nn.sigmoid(_dot(h, wug_ref))
    u_ref[...] = u
    lanes = pl.ds(pl.multiple_of(j * cw, cw), cw)
    sz_ref[:, lanes] = _silu(_dot(h, wzb_ref))
    acc = cb_ref[...] + cw_ref[CONV_B - 1:CONV_B, :] * u
    for k in range(CONV_B - 1):
        acc = acc + cw_ref[k:k + 1, :] * st_ref[k]
    conv_ref[:, lanes] = acc

    @pl.when(j == pl.num_programs(1) - 1)
    def _():
        _layernorm_gate(conv_ref, sz_ref, lng_ref, lnb_ref, yb_ref)


def _branch_b_sample(h, w_in, conv_w, conv_b, ln_g, ln_b, w_b, col0, *, tm, cw, state):
    m, d = h.shape
    n_chunks = w_b // cw
    col = lambda k: (lambda i, j: (0, col0 // cw + k * n_chunks + j))
    in_specs = [pl.BlockSpec((tm, d), lambda i, j: (i, 0))]
    in_specs += [pl.BlockSpec((d, cw), col(k)) for k in (0, 1, 2)]
    in_specs += [pl.BlockSpec((CONV_B, cw), lambda i, j: (0, j)),
                 pl.BlockSpec((1, cw), lambda i, j: (0, j)),
                 pl.BlockSpec((1, w_b), lambda i, j: (0, 0)),
                 pl.BlockSpec((1, w_b), lambda i, j: (0, 0)),
                 pl.BlockSpec((CONV_B - 1, tm, cw), lambda i, j: (0, i, j))]
    return pl.pallas_call(
        _branch_b_sample_kernel, grid=(m // tm, n_chunks), in_specs=in_specs,
        out_specs=[pl.BlockSpec((tm, w_b), lambda i, j: (i, 0)),
                   pl.BlockSpec((tm, cw), lambda i, j: (i, j))],
        out_shape=[jax.ShapeDtypeStruct((m, w_b), BF16), jax.ShapeDtypeStruct((m, w_b), F32)],
        scratch_shapes=[pltpu.VMEM((tm, w_b), F32), pltpu.VMEM((tm, w_b), F32)],
        compiler_params=_params(2),
    )(h, w_in, w_in, w_in, conv_w, conv_b.reshape(1, w_b), ln_g.reshape(1, w_b), ln_b.reshape(1, w_b),
      jnp.transpose(state, (1, 0, 2)))


def _merge_kernel(h_ref, ya_ref, yb_ref, wga_ref, wgb_ref, woa_ref, wob_ref, mix_ref):
    h = h_ref[...]
    mix = (jax.nn.sigmoid(_dot(h, wga_ref)) * _dot(ya_ref[...], woa_ref)
           + jax.nn.sigmoid(_dot(h, wgb_ref)) * _dot(yb_ref[...], wob_ref))
    mix_ref[...] = mix.astype(mix_ref.dtype)


def _merge(h, ya, yb, w_in, w_out_a, w_out_b, col0, *, tm, cn):
    m, d = h.shape
    w = ya.shape[1]
    n_chunks = d // cn
    gcol = lambda k: (lambda i, j: (0, col0 // cn + k * n_chunks + j))
    return pl.pallas_call(
        _merge_kernel,
        grid=(m // tm, n_chunks),
        in_specs=[pl.BlockSpec((tm, d), lambda i, j: (i, 0)),
                  pl.BlockSpec((tm, w), lambda i, j: (i, 0)),
                  pl.BlockSpec((tm, w), lambda i, j: (i, 0)),
                  pl.BlockSpec((d, cn), gcol(0)),
                  pl.BlockSpec((d, cn), gcol(1)),
                  pl.BlockSpec((w, cn), lambda i, j: (0, j)),
                  pl.BlockSpec((w, cn), lambda i, j: (0, j))],
        out_specs=pl.BlockSpec((tm, cn), lambda i, j: (i, j)),
        out_shape=jax.ShapeDtypeStruct((m, d), BF16),
        compiler_params=_params(2),
    )(h, ya, yb, w_in, w_in, w_out_a, w_out_b)


def _out_kernel(mix_ref, x_ref, wo_ref, g_ref, y_ref, ssq_ref):
    j = pl.program_id(1)
    cn = wo_ref.shape[1]
    o = x_ref[...] + _dot(mix_ref[...], wo_ref)
    y_ref[:, pl.ds(pl.multiple_of(j * cn, cn), cn)] = o
    part = jnp.sum(o * o, axis=-1, keepdims=True)

    @pl.when(j == 0)
    def _():
        ssq_ref[...] = part

    @pl.when(j != 0)
    def _():
        ssq_ref[...] += part

    @pl.when(j == pl.num_programs(1) - 1)
    def _():
        scale = lax.rsqrt(ssq_ref[...] / y_ref.shape[1] + EPS)
        y_ref[...] = y_ref[...] * scale * g_ref[...]


def _out_proj(mix, x, w_o, gain, *, tm, cn):
    m, d = x.shape
    return pl.pallas_call(
        _out_kernel,
        grid=(m // tm, d // cn),
        in_specs=[pl.BlockSpec((tm, d), lambda i, j: (i, 0)),
                  pl.BlockSpec((tm, cn), lambda i, j: (i, j)),
                  pl.BlockSpec((d, cn), lambda i, j: (0, j)),
                  pl.BlockSpec((1, d), lambda i, j: (0, 0))],
        out_specs=pl.BlockSpec((tm, d), lambda i, j: (i, 0)),
        out_shape=jax.ShapeDtypeStruct((m, d), F32),
        scratch_shapes=[pltpu.VMEM((tm, 1), F32)],
        compiler_params=_params(2),
    )(mix, x, w_o, gain.reshape(1, d))


def _tiles(m):
    if m >= 1024:
        return dict(norm=256, a=1024, b=512, merge=512, out=512)
    return dict(norm=m, a=m, b=m, merge=m, out=m)


def _layer(x, w, *, seq_len=None, state_a=None, state_b=None):
    m, d = x.shape
    w_a = w["conv_a_w"].shape[1]
    w_b = w["conv_b_w"].shape[1]
    t = _tiles(m)
    h = _rmsnorm_cast(x, w["norm_gain"], t["norm"])
    ya, a_tail = _branch_a(h, w["w_in"], w["conv_a_w"], w["conv_a_b"], w_a,
                           tm=t["a"], cw=256, seq_len=seq_len, state=state_a)
    b_args = (h, w["w_in"], w["conv_b_w"], w["conv_b_b"], w["ln_b_gain"], w["ln_b_bias"], w_b, 4 * w_a)
    if state_b is None:
        yb, b_tail = _branch_b_prompt(*b_args, tm=t["b"], cw=256, seq_len=seq_len)
    else:
        yb, b_tail = _branch_b_sample(*b_args, tm=t["b"], cw=256, state=state_b)
    mix = _merge(h, ya, yb, w["w_in"], w["w_out_a"], w["w_out_b"], 4 * w_a + 3 * w_b,
                 tm=t["merge"], cn=512)
    y = _out_proj(mix, x, w["w_o"], w["final_norm_gain"], tm=t["out"], cn=512)
    return y, a_tail, b_tail


def kernel(x_prompt, x_sample, state_conv_a, state_conv_b, norm_gain, w_in, conv_a_w, conv_a_b, w_out_a,
           conv_b_w, conv_b_b, ln_b_gain, ln_b_bias, w_out_b, w_o, final_norm_gain):
    depth = norm_gain.shape[0]
    assert depth == 1, "the output projection kernel applies the final norm: one layer only"
    batch, seq, d = x_prompt.shape
    dec_batch, dec_seq, _ = x_sample.shape
    assert dec_seq == 1
    w = dict(norm_gain=norm_gain[0], w_in=w_in[0].astype(BF16), conv_a_w=conv_a_w[0], conv_a_b=conv_a_b[0],
             w_out_a=w_out_a[0].astype(BF16), conv_b_w=conv_b_w[0], conv_b_b=conv_b_b[0],
             ln_b_gain=ln_b_gain[0], ln_b_bias=ln_b_bias[0], w_out_b=w_out_b[0].astype(BF16),
             w_o=w_o[0].astype(BF16), final_norm_gain=final_norm_gain)

    yp, a_tail, b_tail = _layer(x_prompt.reshape(batch * seq, d), w, seq_len=seq)
    new_a_prompt = a_tail[:, HIST_A - (CONV_A - 1):, :]
    new_b_prompt = b_tail[:, HIST_B - (CONV_B - 1):, :]

    sa, sb = state_conv_a[0], state_conv_b[0]
    ys, cv_s, u_s = _layer(x_sample.reshape(dec_batch, d), w, state_a=sa, state_b=sb)
    new_a_sample = jnp.concatenate([sa[:, 1:, :], cv_s[:, None, :]], axis=1)
    new_b_sample = jnp.concatenate([sb[:, 1:, :], u_s[:, None, :]], axis=1)

    return (yp.reshape(batch, seq, d), ys.reshape(dec_batch, dec_seq, d),
            new_a_prompt[None], new_b_prompt[None], new_a_sample[None], new_b_sample[None])
```

```python
import functools

import jax
import jax.numpy as jnp
from jax import lax
from jax.experimental import pallas as pl
from jax.experimental.pallas import tpu as pltpu

EPS = 1e-6
CONV_A = 3
CONV_B = 31
SUBLANES = 8
HIST_A = SUBLANES
HIST_B = 4 * SUBLANES
VMEM_LIMIT = 56 * 1024 * 1024

F32 = jnp.float32
BF16 = jnp.bfloat16


def _params(n_axes):
    return pltpu.CompilerParams(
        dimension_semantics=("arbitrary",) * n_axes, vmem_limit_bytes=VMEM_LIMIT)


def _dot(a, w_ref):
    return jnp.dot(a, w_ref[...], preferred_element_type=F32)


def _silu(x):
    return x * jax.nn.sigmoid(x)


def _rmsnorm_kernel(x_ref, g_ref, h_ref):
    x = x_ref[...]
    ms = jnp.mean(x * x, axis=-1, keepdims=True)
    h_ref[...] = (x * lax.rsqrt(ms + EPS) * g_ref[...]).astype(h_ref.dtype)


def _rmsnorm_cast(x, gain, tm):
    m, d = x.shape
    return pl.pallas_call(
        _rmsnorm_kernel,
        grid=(m // tm,),
        in_specs=[pl.BlockSpec((tm, d), lambda i: (i, 0)),
                  pl.BlockSpec((1, d), lambda i: (0, 0))],
        out_specs=pl.BlockSpec((tm, d), lambda i: (i, 0)),
        out_shape=jax.ShapeDtypeStruct((m, d), BF16),
        compiler_params=_params(1),
    )(x, gain.reshape(1, d))


def _branch_a_prompt_kernel(h_ref, wb_ref, wc_ref, wv_ref, wz_ref, cw_ref, cb_ref,
                            ya_ref, cvlast_ref, buf_ref, carry_ref, *, tiles_per_seq):
    m, j = pl.program_id(0), pl.program_id(1)
    tm = h_ref.shape[0]
    h = h_ref[...]
    cv = _dot(h, wc_ref) * _dot(h, wv_ref)

    @pl.when(m % tiles_per_seq == 0)
    def _():
        buf_ref[0:HIST_A, :] = jnp.zeros((HIST_A, cv.shape[1]), F32)

    @pl.when(m % tiles_per_seq != 0)
    def _():
        buf_ref[0:HIST_A, :] = carry_ref[j]

    buf_ref[HIST_A:, :] = cv
    last = cv[tm - HIST_A:, :]
    carry_ref[j] = last
    cvlast_ref[0] = last
    conv = (cw_ref[0:1, :] * buf_ref[HIST_A - 2:HIST_A - 2 + tm, :]
            + cw_ref[1:2, :] * buf_ref[HIST_A - 1:HIST_A - 1 + tm, :]
            + cw_ref[2:3, :] * cv + cb_ref[...])
    ya = _dot(h, wb_ref) * conv * _silu(_dot(h, wz_ref))
    ya_ref[...] = ya.astype(ya_ref.dtype)


def _branch_a_sample_kernel(h_ref, wb_ref, wc_ref, wv_ref, wz_ref, cw_ref, cb_ref, s0_ref, s1_ref,
                            ya_ref, cv_ref):
    h = h_ref[...]
    cv = _dot(h, wc_ref) * _dot(h, wv_ref)
    cv_ref[...] = cv
    conv = (cw_ref[0:1, :] * s0_ref[...] + cw_ref[1:2, :] * s1_ref[...]
            + cw_ref[2:3, :] * cv + cb_ref[...])
    ya = _dot(h, wb_ref) * conv * _silu(_dot(h, wz_ref))
    ya_ref[...] = ya.astype(ya_ref.dtype)


def _branch_a(h, w_in, conv_w, conv_b, w_a, *, tm, cw, seq_len=None, state=None):
    m, d = h.shape
    n_chunks = w_a // cw
    col = lambda k: (lambda i, j: (0, k * n_chunks + j))
    in_specs = [pl.BlockSpec((tm, d), lambda i, j: (i, 0))]
    in_specs += [pl.BlockSpec((d, cw), col(k)) for k in (0, 1, 2, 3)]
    in_specs += [pl.BlockSpec((CONV_A, cw), lambda i, j: (0, j)),
                 pl.BlockSpec((1, cw), lambda i, j: (0, j))]
    args = [h, w_in, w_in, w_in, w_in, conv_w, conv_b.reshape(1, w_a)]
    ya_spec = pl.BlockSpec((tm, cw), lambda i, j: (i, j))
    ya_shape = jax.ShapeDtypeStruct((m, w_a), BF16)
    if state is None:
        tiles_per_seq = seq_len // tm
        kernel = functools.partial(_branch_a_prompt_kernel, tiles_per_seq=tiles_per_seq)
        out_specs = [ya_spec, pl.BlockSpec((1, HIST_A, cw), lambda i, j: (i, 0, j))]
        out_shape = [ya_shape, jax.ShapeDtypeStruct((m // tm, HIST_A, w_a), F32)]
        scratch = [pltpu.VMEM((tm + HIST_A, cw), F32), pltpu.VMEM((n_chunks, HIST_A, cw), F32)]
    else:
        kernel = _branch_a_sample_kernel
        in_specs += [pl.BlockSpec((tm, cw), lambda i, j: (i, j))] * 2
        args += [state[:, 0, :], state[:, 1, :]]
        out_specs = [ya_spec, pl.BlockSpec((tm, cw), lambda i, j: (i, j))]
        out_shape = [ya_shape, jax.ShapeDtypeStruct((m, w_a), F32)]
        scratch = []
    ya, tail = pl.pallas_call(
        kernel, grid=(m // tm, n_chunks), in_specs=in_specs, out_specs=out_specs,
        out_shape=out_shape, scratch_shapes=scratch, compiler_params=_params(2),
    )(*args)
    if state is None:
        tail = tail[tiles_per_seq - 1::tiles_per_seq]
    return ya, tail


LN_ROWS = 16
LN_UNROLL = 8


def _layernorm_gate(conv_ref, sz_ref, lng_ref, lnb_ref, yb_ref):
    tm, w = conv_ref.shape
    g = lng_ref[...]
    b = lnb_ref[...]

    def body(i, _):
        r0 = pl.multiple_of(i * LN_ROWS, LN_ROWS)
        x = conv_ref[pl.ds(r0, LN_ROWS), :]
        mu = jnp.mean(x, axis=-1, keepdims=True)
        xc = x - mu
        var = jnp.mean(xc * xc, axis=-1, keepdims=True)
        r = xc * lax.rsqrt(var + EPS) * g + b
        yb_ref[pl.ds(r0, LN_ROWS), :] = (_silu(r) * sz_ref[pl.ds(r0, LN_ROWS), :]).astype(yb_ref.dtype)
        return 0

    lax.fori_loop(0, tm // LN_ROWS, body, 0, unroll=LN_UNROLL)


CONV_ROWS = 32


MXU_K = 256
ANCHOR_TILE = (16, 128)


def _anchored_weights(w_ref, anchors, never):
    ar, ac = ANCHOR_TILE
    pieces = []
    for i in range(w_ref.shape[0] // MXU_K):
        r0 = i * MXU_K
        if i not in anchors:
            pieces.append(w_ref[r0:r0 + MXU_K, :])
            continue
        first = jnp.where(never, anchors[i].astype(w_ref.dtype), w_ref[r0:r0 + ar, 0:ac])
        top = jnp.concatenate([first, w_ref[r0:r0 + ar, ac:]], axis=1)
        pieces.append(jnp.concatenate([top, w_ref[r0 + ar:r0 + MXU_K, :]], axis=0))
    return jnp.concatenate(pieces, axis=0)


def _fold_to_anchor(x):
    ar, ac = ANCHOR_TILE
    tiles = [x[r:r + ar, c:c + ac] for r in range(0, x.shape[0], ar) for c in range(0, x.shape[1], ac)]
    out = tiles[0]
    for t in tiles[1:]:
        out = out + t
    return out


def _branch_b_prompt_kernel(h_ref, wua_ref, wug_ref, wzb_ref, cw_ref, cb_ref, lng_ref, lnb_ref,
                            yb_ref, ulast_ref, ua_ref, ug_ref, zb_ref, ush_ref, carry_ref, conv_ref, sz_ref,
                            *, tiles_per_seq, n_chunks, n_steps):
    s = pl.program_id(0)
    prev = jnp.maximum(s - 1, 0)
    mp, jp = prev // n_chunks, prev % n_chunks
    tm = h_ref.shape[0]
    cw = cw_ref.shape[1]
    n_rows = tm + HIST_B
    lanes_p = pl.ds(pl.multiple_of(jp * cw, cw), cw)

    @pl.when(s == 0)
    def _():
        for ref in (ua_ref, ug_ref, zb_ref, carry_ref):
            ref[...] = jnp.zeros(ref.shape, F32)

    u = ua_ref[...] * jax.nn.sigmoid(ug_ref[...])
    szv = _silu(zb_ref[...])
    sz_ref[:, lanes_p] = szv
    hist = jnp.where(mp % tiles_per_seq != 0, carry_ref[jp], 0.0)
    tail = u[tm - HIST_B:, :]
    carry_ref[jp] = tail
    ulast_ref[0] = tail
    staged = jnp.concatenate([hist, u], axis=0)
    ush_ref[0] = staged
    for r in range(1, SUBLANES):
        ush_ref[r] = pltpu.roll(staged, n_rows - r, axis=0)

    first = HIST_B - (CONV_B - 1)

    def conv_rows(r0):
        acc = cb_ref[...]
        for k in range(CONV_B):
            r, q = (first + k) % SUBLANES, (first + k) // SUBLANES
            acc = acc + cw_ref[k:k + 1, :] * ush_ref[r, r0 + q * SUBLANES:r0 + q * SUBLANES + CONV_ROWS, :]
        conv_ref[r0:r0 + CONV_ROWS, lanes_p] = acc
        return _fold_to_anchor(acc)

    deps = [conv_rows(r0) for r0 in range(0, tm, CONV_ROWS)]
    deps.append(_fold_to_anchor(szv[:, 0:ANCHOR_TILE[1]]))

    n_slabs = h_ref.shape[1] // MXU_K
    first_slab, last_slab = 6, 3 * n_slabs - 2
    anchors = [{}, {}, {}]
    for i, dep in enumerate(deps):
        slab = first_slab + (last_slab - first_slab) * i // (len(deps) - 1)
        anchors[slab // n_slabs][slab % n_slabs] = dep
    never = s < 0
    h = h_ref[...]
    ua_ref[...] = jnp.dot(h, _anchored_weights(wua_ref, anchors[0], never), preferred_element_type=F32)
    ug_ref[...] = jnp.dot(h, _anchored_weights(wug_ref, anchors[1], never), preferred_element_type=F32)
    zb_ref[...] = jnp.dot(h, _anchored_weights(wzb_ref, anchors[2], never), preferred_element_type=F32)

    @pl.when(jnp.logical_and(s > 0, jp == n_chunks - 1))
    def _():
        _layernorm_gate(conv_ref, sz_ref, lng_ref, lnb_ref, yb_ref)


def _branch_b_prompt(h, w_in, conv_w, conv_b, ln_g, ln_b, w_b, col0, *, tm, cw, seq_len):
    m, d = h.shape
    n_chunks = w_b // cw
    n_tiles = m // tm
    n_steps = n_tiles * n_chunks
    tiles_per_seq = seq_len // tm
    cur = lambda s: jnp.minimum(s, n_steps - 1)
    prev = lambda s: jnp.maximum(s - 1, 0)
    col = lambda k: (lambda s: (0, col0 // cw + k * n_chunks + cur(s) % n_chunks))
    in_specs = [pl.BlockSpec((tm, d), lambda s: (cur(s) // n_chunks, 0))]
    in_specs += [pl.BlockSpec((d, cw), col(k)) for k in (0, 1, 2)]
    in_specs += [pl.BlockSpec((CONV_B, cw), lambda s: (0, prev(s) % n_chunks)),
                 pl.BlockSpec((1, cw), lambda s: (0, prev(s) % n_chunks)),
                 pl.BlockSpec((1, w_b), lambda s: (0, 0)),
                 pl.BlockSpec((1, w_b), lambda s: (0, 0))]
    kernel = functools.partial(_branch_b_prompt_kernel, tiles_per_seq=tiles_per_seq,
                               n_chunks=n_chunks, n_steps=n_steps)
    yb, tail = pl.pallas_call(
        kernel, grid=(n_steps + 1,), in_specs=in_specs,
        out_specs=[pl.BlockSpec((tm, w_b), lambda s: (prev(s) // n_chunks, 0)),
                   pl.BlockSpec((1, HIST_B, cw), lambda s: (prev(s) // n_chunks, 0, prev(s) % n_chunks))],
        out_shape=[jax.ShapeDtypeStruct((m, w_b), BF16),
                   jax.ShapeDtypeStruct((n_tiles, HIST_B, w_b), F32)],
        scratch_shapes=[pltpu.VMEM((tm, cw), F32),
                        pltpu.VMEM((tm, cw), F32),
                        pltpu.VMEM((tm, cw), F32),
                        pltpu.VMEM((SUBLANES, tm + HIST_B, cw), F32),
                        pltpu.VMEM((n_chunks, HIST_B, cw), F32),
                        pltpu.VMEM((tm, w_b), F32),
                        pltpu.VMEM((tm, w_b), F32)],
        compiler_params=_params(1),
    )(h, w_in, w_in, w_in, conv_w, conv_b.reshape(1, w_b), ln_g.reshape(1, w_b), ln_b.reshape(1, w_b))
    return yb, tail[tiles_per_seq - 1::tiles_per_seq]


def _branch_b_sample_kernel(h_ref, wua_ref, wug_ref, wzb_ref, cw_ref, cb_ref, lng_ref, lnb_ref, st_ref,
                            yb_ref, u_ref, conv_ref, sz_ref):
    j = pl.program_id(1)
    cw = cw_ref.shape[1]
    h = h_ref[...]
    u = _dot(h, wua_ref) * jax.nn.sigmoid(_dot(h, wug_ref))
    u_ref[...] = u
    lanes = pl.ds(pl.multiple_of(j * cw, cw), cw)
    sz_ref[:, lanes] = _silu(_dot(h, wzb_ref))
    acc = cb_ref[...] + cw_ref[CONV_B - 1:CONV_B, :] * u
    for k in range(CONV_B - 1):
        acc = acc + cw_ref[k:k + 1, :] * st_ref[k]
    conv_ref[:, lanes] = acc

    @pl.when(j == pl.num_programs(1) - 1)
    def _():
        _layernorm_gate(conv_ref, sz_ref, lng_ref, lnb_ref, yb_ref)


def _branch_b_sample(h, w_in, conv_w, conv_b, ln_g, ln_b, w_b, col0, *, tm, cw, state):
    m, d = h.shape
    n_chunks = w_b // cw
    col = lambda k: (lambda i, j: (0, col0 // cw + k * n_chunks + j))
    in_specs = [pl.BlockSpec((tm, d), lambda i, j: (i, 0))]
    in_specs += [pl.BlockSpec((d, cw), col(k)) for k in (0, 1, 2)]
    in_specs += [pl.BlockSpec((CONV_B, cw), lambda i, j: (0, j)),
                 pl.BlockSpec((1, cw), lambda i, j: (0, j)),
                 pl.BlockSpec((1, w_b), lambda i, j: (0, 0)),
                 pl.BlockSpec((1, w_b), lambda i, j: (0, 0)),
                 pl.BlockSpec((CONV_B - 1, tm, cw), lambda i, j: (0, i, j))]
    return pl.pallas_call(
        _branch_b_sample_kernel, grid=(m // tm, n_chunks), in_specs=in_specs,
        out_specs=[pl.BlockSpec((tm, w_b), lambda i, j: (i, 0)),
                   pl.BlockSpec((tm, cw), lambda i, j: (i, j))],
        out_shape=[jax.ShapeDtypeStruct((m, w_b), BF16), jax.ShapeDtypeStruct((m, w_b), F32)],
        scratch_shapes=[pltpu.VMEM((tm, w_b), F32), pltpu.VMEM((tm, w_b), F32)],
        compiler_params=_params(2),
    )(h, w_in, w_in, w_in, conv_w, conv_b.reshape(1, w_b), ln_g.reshape(1, w_b), ln_b.reshape(1, w_b),
      jnp.transpose(state, (1, 0, 2)))


def _merge_kernel(h_ref, ya_ref, yb_ref, wga_ref, wgb_ref, woa_ref, wob_ref, mix_ref):
    h = h_ref[...]
    mix = (jax.nn.sigmoid(_dot(h, wga_ref)) * _dot(ya_ref[...], woa_ref)
           + jax.nn.sigmoid(_dot(h, wgb_ref)) * _dot(yb_ref[...], wob_ref))
    mix_ref[...] = mix.astype(mix_ref.dtype)


def _merge(h, ya, yb, w_in, w_out_a, w_out_b, col0, *, tm, cn):
    m, d = h.shape
    w = ya.shape[1]
    n_chunks = d // cn
    gcol = lambda k: (lambda i, j: (0, col0 // cn + k * n_chunks + j))
    return pl.pallas_call(
        _merge_kernel,
        grid=(m // tm, n_chunks),
        in_specs=[pl.BlockSpec((tm, d), lambda i, j: (i, 0)),
                  pl.BlockSpec((tm, w), lambda i, j: (i, 0)),
                  pl.BlockSpec((tm, w), lambda i, j: (i, 0)),
                  pl.BlockSpec((d, cn), gcol(0)),
                  pl.BlockSpec((d, cn), gcol(1)),
                  pl.BlockSpec((w, cn), lambda i, j: (0, j)),
                  pl.BlockSpec((w, cn), lambda i, j: (0, j))],
        out_specs=pl.BlockSpec((tm, cn), lambda i, j: (i, j)),
        out_shape=jax.ShapeDtypeStruct((m, d), BF16),
        compiler_params=_params(2),
    )(h, ya, yb, w_in, w_in, w_out_a, w_out_b)


def _out_kernel(mix_ref, x_ref, wo_ref, g_ref, y_ref, ssq_ref):
    j = pl.program_id(1)
    cn = wo_ref.shape[1]
    o = x_ref[...] + _dot(mix_ref[...], wo_ref)
    y_ref[:, pl.ds(pl.multiple_of(j * cn, cn), cn)] = o
    part = jnp.sum(o * o, axis=-1, keepdims=True)

    @pl.when(j == 0)
    def _():
        ssq_ref[...] = part

    @pl.when(j != 0)
    def _():
        ssq_ref[...] += part

    @pl.when(j == pl.num_programs(1) - 1)
    def _():
        scale = lax.rsqrt(ssq_ref[...] / y_ref.shape[1] + EPS)
        y_ref[...] = y_ref[...] * scale * g_ref[...]


def _out_proj(mix, x, w_o, gain, *, tm, cn):
    m, d = x.shape
    return pl.pallas_call(
        _out_kernel,
        grid=(m // tm, d // cn),
        in_specs=[pl.BlockSpec((tm, d), lambda i, j: (i, 0)),
                  pl.BlockSpec((tm, cn), lambda i, j: (i, j)),
                  pl.BlockSpec((d, cn), lambda i, j: (0, j)),
                  pl.BlockSpec((1, d), lambda i, j: (0, 0))],
        out_specs=pl.BlockSpec((tm, d), lambda i, j: (i, 0)),
        out_shape=jax.ShapeDtypeStruct((m, d), F32),
        scratch_shapes=[pltpu.VMEM((tm, 1), F32)],
        compiler_params=_params(2),
    )(mix, x, w_o, gain.reshape(1, d))


def _tiles(m):
    if m >= 1024:
        return dict(norm=256, a=1024, b=512, merge=512, out=512)
    return dict(norm=m, a=m, b=m, merge=m, out=m)


def _layer(x, w, *, seq_len=None, state_a=None, state_b=None):
    m, d = x.shape
    w_a = w["conv_a_w"].shape[1]
    w_b = w["conv_b_w"].shape[1]
    t = _tiles(m)
    h = _rmsnorm_cast(x, w["norm_gain"], t["norm"])
    ya, a_tail = _branch_a(h, w["w_in"], w["conv_a_w"], w["conv_a_b"], w_a,
                           tm=t["a"], cw=256, seq_len=seq_len, state=state_a)
    b_args = (h, w["w_in"], w["conv_b_w"], w["conv_b_b"], w["ln_b_gain"], w["ln_b_bias"], w_b, 4 * w_a)
    if state_b is None:
        yb, b_tail = _branch_b_prompt(*b_args, tm=t["b"], cw=256, seq_len=seq_len)
    else:
        yb, b_tail = _branch_b_sample(*b_args, tm=t["b"], cw=256, state=state_b)
    mix = _merge(h, ya, yb, w["w_in"], w["w_out_a"], w["w_out_b"], 4 * w_a + 3 * w_b,
                 tm=t["merge"], cn=512)
    y = _out_proj(mix, x, w["w_o"], w["final_norm_gain"], tm=t["out"], cn=512)
    return y, a_tail, b_tail


def kernel(x_prompt, x_sample, state_conv_a, state_conv_b, norm_gain, w_in, conv_a_w, conv_a_b, w_out_a,
           conv_b_w, conv_b_b, ln_b_gain, ln_b_bias, w_out_b, w_o, final_norm_gain):
    depth = norm_gain.shape[0]
    assert depth == 1, "the output projection kernel applies the final norm: one layer only"
    batch, seq, d = x_prompt.shape
    dec_batch, dec_seq, _ = x_sample.shape
    assert dec_seq == 1
    w = dict(norm_gain=norm_gain[0], w_in=w_in[0].astype(BF16), conv_a_w=conv_a_w[0], conv_a_b=conv_a_b[0],
             w_out_a=w_out_a[0].astype(BF16), conv_b_w=conv_b_w[0], conv_b_b=conv_b_b[0],
             ln_b_gain=ln_b_gain[0], ln_b_bias=ln_b_bias[0], w_out_b=w_out_b[0].astype(BF16),
             w_o=w_o[0].astype(BF16), final_norm_gain=final_norm_gain)

    yp, a_tail, b_tail = _layer(x_prompt.reshape(batch * seq, d), w, seq_len=seq)
    new_a_prompt = a_tail[:, HIST_A - (CONV_A - 1):, :]
    new_b_prompt = b_tail[:, HIST_B - (CONV_B - 1):, :]

    sa, sb = state_conv_a[0], state_conv_b[0]
    ys, cv_s, u_s = _layer(x_sample.reshape(dec_batch, d), w, state_a=sa, state_b=sb)
    new_a_sample = jnp.concatenate([sa[:, 1:, :], cv_s[:, None, :]], axis=1)
    new_b_sample = jnp.concatenate([sb[:, 1:, :], u_s[:, None, :]], axis=1)

    return (yp.reshape(batch, seq, d), ys.reshape(dec_batch, dec_seq, d),
            new_a_prompt[None], new_b_prompt[None], new_a_sample[None], new_b_sample[None])
```

```python
import functools

import jax
import jax.numpy as jnp
from jax import lax
from jax.experimental import pallas as pl
from jax.experimental.pallas import tpu as pltpu

EPS = 1e-6
CONV_A = 3
CONV_B = 31
SUBLANES = 8
HIST_A = SUBLANES
HIST_B = 4 * SUBLANES
VMEM_LIMIT = 60 * 1024 * 1024

F32 = jnp.float32
BF16 = jnp.bfloat16


def _params(n_axes):
    return pltpu.CompilerParams(
        dimension_semantics=("arbitrary",) * n_axes, vmem_limit_bytes=VMEM_LIMIT)


def _dot(a, w_ref):
    return jnp.dot(a, w_ref[...], preferred_element_type=F32)


def _silu(x):
    return x * jax.nn.sigmoid(x)


def _rmsnorm_kernel(x_ref, g_ref, h_ref):
    x = x_ref[...]
    ms = jnp.mean(x * x, axis=-1, keepdims=True)
    h_ref[...] = (x * lax.rsqrt(ms + EPS) * g_ref[...]).astype(h_ref.dtype)


def _rmsnorm_cast(x, gain, tm):
    m, d = x.shape
    return pl.pallas_call(
        _rmsnorm_kernel,
        grid=(m // tm,),
        in_specs=[pl.BlockSpec((tm, d), lambda i: (i, 0)),
                  pl.BlockSpec((1, d), lambda i: (0, 0))],
        out_specs=pl.BlockSpec((tm, d), lambda i: (i, 0)),
        out_shape=jax.ShapeDtypeStruct((m, d), BF16),
        compiler_params=_params(1),
    )(x, gain.reshape(1, d))


def _cast_slab(t, n_steps, jobs, srcs, dsts, inbufs, outbufs, in_sems, out_sems):
    def read(i, step, slot):
        rows, col0, cols = jobs[i]
        src = srcs[i].at[pl.ds(pl.multiple_of(step * rows, rows), rows), pl.ds(col0, cols)]
        return pltpu.make_async_copy(src, inbufs[i].at[slot], in_sems.at[slot, i])

    def write(i, step):
        rows = jobs[i][0]
        dst = dsts[i].at[pl.ds(pl.multiple_of(step * rows, rows), rows), :]
        return pltpu.make_async_copy(outbufs[i], dst, out_sems.at[i])

    n = len(jobs)
    slot = t % 2

    @pl.when(t == 0)
    def _():
        for i in range(n):
            read(i, 0, 0).start()

    for i in range(n):
        read(i, t, slot).wait()

    @pl.when(t + 1 < n_steps)
    def _():
        for i in range(n):
            read(i, t + 1, 1 - slot).start()

    @pl.when(t > 0)
    def _():
        for i in range(n):
            write(i, t - 1).wait()

    for i in range(n):
        outbufs[i][...] = inbufs[i][slot].astype(outbufs[i].dtype)
    for i in range(n):
        write(i, t).start()

    @pl.when(t == n_steps - 1)
    def _():
        for i in range(n):
            write(i, t).wait()


def _branch_a_prompt_kernel(h_ref, wb_ref, wc_ref, wv_ref, wz_ref, cw_ref, cb_ref, *rest,
                            tiles_per_seq, jobs):
    n = len(jobs)
    srcs, rest = rest[:n], rest[n:]
    ya_ref, cvlast_ref = rest[:2]
    dsts, rest = rest[2:2 + n], rest[2 + n:]
    buf_ref, carry_ref = rest[:2]
    inbufs, outbufs = rest[2:2 + n], rest[2 + n:2 + 2 * n]
    in_sems, out_sems = rest[2 + 2 * n:]
    m, j = pl.program_id(0), pl.program_id(1)
    n_steps = pl.num_programs(0) * pl.num_programs(1)
    _cast_slab(m * pl.num_programs(1) + j, n_steps, jobs, srcs, dsts, inbufs, outbufs, in_sems, out_sems)
    tm = h_ref.shape[0]
    h = h_ref[...]
    cv = _dot(h, wc_ref) * _dot(h, wv_ref)

    @pl.when(m % tiles_per_seq == 0)
    def _():
        buf_ref[0:HIST_A, :] = jnp.zeros((HIST_A, cv.shape[1]), F32)

    @pl.when(m % tiles_per_seq != 0)
    def _():
        buf_ref[0:HIST_A, :] = carry_ref[j]

    buf_ref[HIST_A:, :] = cv
    last = cv[tm - HIST_A:, :]
    carry_ref[j] = last
    cvlast_ref[0] = last
    conv = (cw_ref[0:1, :] * buf_ref[HIST_A - 2:HIST_A - 2 + tm, :]
            + cw_ref[1:2, :] * buf_ref[HIST_A - 1:HIST_A - 1 + tm, :]
            + cw_ref[2:3, :] * cv + cb_ref[...])
    ya = _dot(h, wb_ref) * conv * _silu(_dot(h, wz_ref))
    ya_ref[...] = ya.astype(ya_ref.dtype)


def _branch_a_sample_kernel(h_ref, wb_ref, wc_ref, wv_ref, wz_ref, cw_ref, cb_ref, s0_ref, s1_ref,
                            ya_ref, cv_ref):
    h = h_ref[...]
    cv = _dot(h, wc_ref) * _dot(h, wv_ref)
    cv_ref[...] = cv
    conv = (cw_ref[0:1, :] * s0_ref[...] + cw_ref[1:2, :] * s1_ref[...]
            + cw_ref[2:3, :] * cv + cb_ref[...])
    ya = _dot(h, wb_ref) * conv * _silu(_dot(h, wz_ref))
    ya_ref[...] = ya.astype(ya_ref.dtype)


def _branch_a(h, w_in, conv_w, conv_b, w_a, *, tm, cw, seq_len=None, state=None, casts=()):
    m, d = h.shape
    n_chunks = w_a // cw
    col = lambda k: (lambda i, j: (0, k * n_chunks + j))
    in_specs = [pl.BlockSpec((tm, d), lambda i, j: (i, 0))]
    in_specs += [pl.BlockSpec((d, cw), col(k)) for k in (0, 1, 2, 3)]
    in_specs += [pl.BlockSpec((CONV_A, cw), lambda i, j: (0, j)),
                 pl.BlockSpec((1, cw), lambda i, j: (0, j))]
    args = [h, w_in, w_in, w_in, w_in, conv_w, conv_b.reshape(1, w_a)]
    ya_spec = pl.BlockSpec((tm, cw), lambda i, j: (i, j))
    ya_shape = jax.ShapeDtypeStruct((m, w_a), BF16)
    if state is None:
        tiles_per_seq = seq_len // tm
        n_steps = (m // tm) * n_chunks
        jobs = tuple((src.shape[0] // n_steps, col0, cols) for src, col0, cols in casts)
        assert all(src.shape[0] == rows * n_steps and rows % 16 == 0
                   for (src, _, _), (rows, _, _) in zip(casts, jobs))
        any_spec = pl.BlockSpec(memory_space=pl.ANY)
        kernel = functools.partial(_branch_a_prompt_kernel, tiles_per_seq=tiles_per_seq, jobs=jobs)
        in_specs += [any_spec] * len(jobs)
        args += [src for src, _, _ in casts]
        out_specs = [ya_spec, pl.BlockSpec((1, HIST_A, cw), lambda i, j: (i, 0, j))] + [any_spec] * len(jobs)
        out_shape = [ya_shape, jax.ShapeDtypeStruct((m // tm, HIST_A, w_a), F32)]
        out_shape += [jax.ShapeDtypeStruct((src.shape[0], cols), BF16) for src, _, cols in casts]
        scratch = [pltpu.VMEM((tm + HIST_A, cw), F32), pltpu.VMEM((n_chunks, HIST_A, cw), F32)]
        scratch += [pltpu.VMEM((2, rows, cols), F32) for rows, _, cols in jobs]
        scratch += [pltpu.VMEM((rows, cols), BF16) for rows, _, cols in jobs]
        scratch += [pltpu.SemaphoreType.DMA((2, len(jobs))), pltpu.SemaphoreType.DMA((len(jobs),))]
    else:
        kernel = _branch_a_sample_kernel
        in_specs += [pl.BlockSpec((tm, cw), lambda i, j: (i, j))] * 2
        args += [state[:, 0, :], state[:, 1, :]]
        out_specs = [ya_spec, pl.BlockSpec((tm, cw), lambda i, j: (i, j))]
        out_shape = [ya_shape, jax.ShapeDtypeStruct((m, w_a), F32)]
        scratch = []
    ya, tail, *cast_out = pl.pallas_call(
        kernel, grid=(m // tm, n_chunks), in_specs=in_specs, out_specs=out_specs,
        out_shape=out_shape, scratch_shapes=scratch, compiler_params=_params(2),
    )(*args)
    if state is None:
        tail = tail[tiles_per_seq - 1::tiles_per_seq]
    return ya, tail, cast_out


LN_ROWS = 16
LN_UNROLL = 8


def _layernorm_gate(conv_ref, sz_ref, lng_ref, lnb_ref, yb_ref):
    tm, w = conv_ref.shape
    g = lng_ref[...]
    b = lnb_ref[...]

    def body(i, _):
        r0 = pl.multiple_of(i * LN_ROWS, LN_ROWS)
        x = conv_ref[pl.ds(r0, LN_ROWS), :]
        mu = jnp.mean(x, axis=-1, keepdims=True)
        xc = x - mu
        var = jnp.mean(xc * xc, axis=-1, keepdims=True)
        r = xc * lax.rsqrt(var + EPS) * g + b
        yb_ref[pl.ds(r0, LN_ROWS), :] = (_silu(r) * sz_ref[pl.ds(r0, LN_ROWS), :]).astype(yb_ref.dtype)
        return 0

    lax.fori_loop(0, tm // LN_ROWS, body, 0, unroll=LN_UNROLL)


CONV_ROWS = 32


MXU_K = 256
ANCHOR_TILE = (16, 128)


def _anchored_weights(w_ref, anchors, never):
    ar, ac = ANCHOR_TILE
    pieces = []
    for i in range(w_ref.shape[0] // MXU_K):
        r0 = i * MXU_K
        if i not in anchors:
            pieces.append(w_ref[r0:r0 + MXU_K, :])
            continue
        first = jnp.where(never, anchors[i].astype(w_ref.dtype), w_ref[r0:r0 + ar, 0:ac])
        top = jnp.concatenate([first, w_ref[r0:r0 + ar, ac:]], axis=1)
        pieces.append(jnp.concatenate([top, w_ref[r0 + ar:r0 + MXU_K, :]], axis=0))
    return jnp.concatenate(pieces, axis=0)


def _fold_to_anchor(x):
    ar, ac = ANCHOR_TILE
    tiles = [x[r:r + ar, c:c + ac] for r in range(0, x.shape[0], ar) for c in range(0, x.shape[1], ac)]
    out = tiles[0]
    for t in tiles[1:]:
        out = out + t
    return out


def _branch_b_prompt_kernel(h_ref, wua_ref, wug_ref, wzb_ref, cw_ref, cb_ref, lng_ref, lnb_ref,
                            yb_ref, ulast_ref, ua_ref, ug_ref, zb_ref, ush_ref, carry_ref, conv_ref, sz_ref,
                            *, tiles_per_seq, n_chunks, n_steps):
    s = pl.program_id(0)
    prev = jnp.maximum(s - 1, 0)
    mp, jp = prev // n_chunks, prev % n_chunks
    tm = h_ref.shape[0]
    cw = cw_ref.shape[1]
    n_rows = tm + HIST_B
    lanes_p = pl.ds(pl.multiple_of(jp * cw, cw), cw)

    @pl.when(s == 0)
    def _():
        for ref in (ua_ref, ug_ref, zb_ref, carry_ref):
            ref[...] = jnp.zeros(ref.shape, F32)

    u = ua_ref[...] * jax.nn.sigmoid(ug_ref[...])
    szv = _silu(zb_ref[...])
    sz_ref[:, lanes_p] = szv
    hist = jnp.where(mp % tiles_per_seq != 0, carry_ref[jp], 0.0)
    tail = u[tm - HIST_B:, :]
    carry_ref[jp] = tail
    ulast_ref[0] = tail
    staged = jnp.concatenate([hist, u], axis=0)
    ush_ref[0] = staged
    for r in range(1, SUBLANES):
        ush_ref[r] = pltpu.roll(staged, n_rows - r, axis=0)

    first = HIST_B - (CONV_B - 1)

    def conv_rows(r0):
        acc = cb_ref[...]
        for k in range(CONV_B):
            r, q = (first + k) % SUBLANES, (first + k) // SUBLANES
            acc = acc + cw_ref[k:k + 1, :] * ush_ref[r, r0 + q * SUBLANES:r0 + q * SUBLANES + CONV_ROWS, :]
        conv_ref[r0:r0 + CONV_ROWS, lanes_p] = acc
        return _fold_to_anchor(acc)

    deps = [conv_rows(r0) for r0 in range(0, tm, CONV_ROWS)]
    deps.append(_fold_to_anchor(szv[:, 0:ANCHOR_TILE[1]]))

    n_slabs = h_ref.shape[1] // MXU_K
    first_slab, last_slab = 6, 3 * n_slabs - 2
    anchors = [{}, {}, {}]
    for i, dep in enumerate(deps):
        slab = first_slab + (last_slab - first_slab) * i // (len(deps) - 1)
        anchors[slab // n_slabs][slab % n_slabs] = dep
    never = s < 0
    h = h_ref[...]
    ua_ref[...] = jnp.dot(h, _anchored_weights(wua_ref, anchors[0], never), preferred_element_type=F32)
    ug_ref[...] = jnp.dot(h, _anchored_weights(wug_ref, anchors[1], never), preferred_element_type=F32)
    zb_ref[...] = jnp.dot(h, _anchored_weights(wzb_ref, anchors[2], never), preferred_element_type=F32)

    @pl.when(jnp.logical_and(s > 0, jp == n_chunks - 1))
    def _():
        _layernorm_gate(conv_ref, sz_ref, lng_ref, lnb_ref, yb_ref)


def _branch_b_prompt(h, w_in, conv_w, conv_b, ln_g, ln_b, w_b, col0, *, tm, cw, seq_len):
    m, d = h.shape
    n_chunks = w_b // cw
    n_tiles = m // tm
    n_steps = n_tiles * n_chunks
    tiles_per_seq = seq_len // tm
    cur = lambda s: jnp.minimum(s, n_steps - 1)
    prev = lambda s: jnp.maximum(s - 1, 0)
    col = lambda k: (lambda s: (0, col0 // cw + k * n_chunks + cur(s) % n_chunks))
    in_specs = [pl.BlockSpec((tm, d), lambda s: (cur(s) // n_chunks, 0))]
    in_specs += [pl.BlockSpec((d, cw), col(k)) for k in (0, 1, 2)]
    in_specs += [pl.BlockSpec((CONV_B, cw), lambda s: (0, prev(s) % n_chunks)),
                 pl.BlockSpec((1, cw), lambda s: (0, prev(s) % n_chunks)),
                 pl.BlockSpec((1, w_b), lambda s: (0, 0)),
                 pl.BlockSpec((1, w_b), lambda s: (0, 0))]
    kernel = functools.partial(_branch_b_prompt_kernel, tiles_per_seq=tiles_per_seq,
                               n_chunks=n_chunks, n_steps=n_steps)
    yb, tail = pl.pallas_call(
        kernel, grid=(n_steps + 1,), in_specs=in_specs,
        out_specs=[pl.BlockSpec((tm, w_b), lambda s: (prev(s) // n_chunks, 0)),
                   pl.BlockSpec((1, HIST_B, cw), lambda s: (prev(s) // n_chunks, 0, prev(s) % n_chunks))],
        out_shape=[jax.ShapeDtypeStruct((m, w_b), BF16),
                   jax.ShapeDtypeStruct((n_tiles, HIST_B, w_b), F32)],
        scratch_shapes=[pltpu.VMEM((tm, cw), F32),
                        pltpu.VMEM((tm, cw), F32),
                        pltpu.VMEM((tm, cw), F32),
                        pltpu.VMEM((SUBLANES, tm + HIST_B, cw), F32),
                        pltpu.VMEM((n_chunks, HIST_B, cw), F32),
                        pltpu.VMEM((tm, w_b), F32),
                        pltpu.VMEM((tm, w_b), F32)],
        compiler_params=_params(1),
    )(h, w_in, w_in, w_in, conv_w, conv_b.reshape(1, w_b), ln_g.reshape(1, w_b), ln_b.reshape(1, w_b))
    return yb, tail[tiles_per_seq - 1::tiles_per_seq]


def _branch_b_sample_kernel(h_ref, wua_ref, wug_ref, wzb_ref, cw_ref, cb_ref, lng_ref, lnb_ref, st_ref,
                            yb_ref, u_ref, conv_ref, sz_ref):
    j = pl.program_id(1)
    cw = cw_ref.shape[1]
    h = h_ref[...]
    u = _dot(h, wua_ref) * jax.nn.sigmoid(_dot(h, wug_ref))
    u_ref[...] = u
    lanes = pl.ds(pl.multiple_of(j * cw, cw), cw)
    sz_ref[:, lanes] = _silu(_dot(h, wzb_ref))
    acc = cb_ref[...] + cw_ref[CONV_B - 1:CONV_B, :] * u
    for k in range(CONV_B - 1):
        acc = acc + cw_ref[k:k + 1, :] * st_ref[k]
    conv_ref[:, lanes] = acc

    @pl.when(j == pl.num_programs(1) - 1)
    def _():
        _layernorm_gate(conv_ref, sz_ref, lng_ref, lnb_ref, yb_ref)


def _branch_b_sample(h, w_in, conv_w, conv_b, ln_g, ln_b, w_b, col0, *, tm, cw, state):
    m, d = h.shape
    n_chunks = w_b // cw
    col = lambda k: (lambda i, j: (0, col0 // cw + k * n_chunks + j))
    in_specs = [pl.BlockSpec((tm, d), lambda i, j: (i, 0))]
    in_specs += [pl.BlockSpec((d, cw), col(k)) for k in (0, 1, 2)]
    in_specs += [pl.BlockSpec((CONV_B, cw), lambda i, j: (0, j)),
                 pl.BlockSpec((1, cw), lambda i, j: (0, j)),
                 pl.BlockSpec((1, w_b), lambda i, j: (0, 0)),
                 pl.BlockSpec((1, w_b), lambda i, j: (0, 0)),
                 pl.BlockSpec((CONV_B - 1, tm, cw), lambda i, j: (0, i, j))]
    return pl.pallas_call(
        _branch_b_sample_kernel, grid=(m // tm, n_chunks), in_specs=in_specs,
        out_specs=[pl.BlockSpec((tm, w_b), lambda i, j: (i, 0)),
                   pl.BlockSpec((tm, cw), lambda i, j: (i, j))],
        out_shape=[jax.ShapeDtypeStruct((m, w_b), BF16), jax.ShapeDtypeStruct((m, w_b), F32)],
        scratch_shapes=[pltpu.VMEM((tm, w_b), F32), pltpu.VMEM((tm, w_b), F32)],
        compiler_params=_params(2),
    )(h, w_in, w_in, w_in, conv_w, conv_b.reshape(1, w_b), ln_g.reshape(1, w_b), ln_b.reshape(1, w_b),
      jnp.transpose(state, (1, 0, 2)))


def _merge_kernel(h_ref, ya_ref, yb_ref, wga_ref, wgb_ref, woa_ref, wob_ref, mix_ref):
    h = h_ref[...]
    mix = (jax.nn.sigmoid(_dot(h, wga_ref)) * _dot(ya_ref[...], woa_ref)
           + jax.nn.sigmoid(_dot(h, wgb_ref)) * _dot(yb_ref[...], wob_ref))
    mix_ref[...] = mix.astype(mix_ref.dtype)


def _merge(h, ya, yb, w_in, w_out_a, w_out_b, col0, *, tm, cn):
    m, d = h.shape
    w = ya.shape[1]
    n_chunks = d // cn
    gcol = lambda k: (lambda i, j: (0, col0 // cn + k * n_chunks + j))
    return pl.pallas_call(
        _merge_kernel,
        grid=(m // tm, n_chunks),
        in_specs=[pl.BlockSpec((tm, d), lambda i, j: (i, 0)),
                  pl.BlockSpec((tm, w), lambda i, j: (i, 0)),
                  pl.BlockSpec((tm, w), lambda i, j: (i, 0)),
                  pl.BlockSpec((d, cn), gcol(0)),
                  pl.BlockSpec((d, cn), gcol(1)),
                  pl.BlockSpec((w, cn), lambda i, j: (0, j)),
                  pl.BlockSpec((w, cn), lambda i, j: (0, j))],
        out_specs=pl.BlockSpec((tm, cn), lambda i, j: (i, j)),
        out_shape=jax.ShapeDtypeStruct((m, d), BF16),
        compiler_params=_params(2),
    )(h, ya, yb, w_in, w_in, w_out_a, w_out_b)


def _out_kernel(mix_ref, x_ref, wo_ref, g_ref, y_ref, ssq_ref):
    j = pl.program_id(1)
    cn = wo_ref.shape[1]
    o = x_ref[...] + _dot(mix_ref[...], wo_ref)
    y_ref[:, pl.ds(pl.multiple_of(j * cn, cn), cn)] = o
    part = jnp.sum(o * o, axis=-1, keepdims=True)

    @pl.when(j == 0)
    def _():
        ssq_ref[...] = part

    @pl.when(j != 0)
    def _():
        ssq_ref[...] += part

    @pl.when(j == pl.num_programs(1) - 1)
    def _():
        scale = lax.rsqrt(ssq_ref[...] / y_ref.shape[1] + EPS)
        y_ref[...] = y_ref[...] * scale * g_ref[...]


def _out_proj(mix, x, w_o, gain, *, tm, cn):
    m, d = x.shape
    return pl.pallas_call(
        _out_kernel,
        grid=(m // tm, d // cn),
        in_specs=[pl.BlockSpec((tm, d), lambda i, j: (i, 0)),
                  pl.BlockSpec((tm, cn), lambda i, j: (i, j)),
                  pl.BlockSpec((d, cn), lambda i, j: (0, j)),
                  pl.BlockSpec((1, d), lambda i, j: (0, 0))],
        out_specs=pl.BlockSpec((tm, d), lambda i, j: (i, 0)),
        out_shape=jax.ShapeDtypeStruct((m, d), F32),
        scratch_shapes=[pltpu.VMEM((tm, 1), F32)],
        compiler_params=_params(2),
    )(mix, x, w_o, gain.reshape(1, d))


def _tiles(m):
    if m >= 1024:
        return dict(norm=256, a=1024, b=512, merge=512, out=512)
    return dict(norm=m, a=m, b=m, merge=m, out=m)


def _layer(x, w, bf16_w=None, *, seq_len=None, state_a=None, state_b=None):
    m, d = x.shape
    w_a = w["conv_a_w"].shape[1]
    w_b = w["conv_b_w"].shape[1]
    t = _tiles(m)
    h = _rmsnorm_cast(x, w["norm_gain"], t["norm"])
    casts = ()
    if bf16_w is None:
        w_in = w["w_in"]
        casts = ((w_in, 4 * w_a, w_in.shape[1] - 4 * w_a), (w["w_out_a"], 0, d), (w["w_out_b"], 0, d),
                 (w["w_o"], 0, d))
    ya, a_tail, cast_out = _branch_a(h, w["w_in_a"], w["conv_a_w"], w["conv_a_b"], w_a,
                                     tm=t["a"], cw=256, seq_len=seq_len, state=state_a, casts=casts)
    if bf16_w is None:
        bf16_w = tuple(cast_out)
    w_rest, w_out_a, w_out_b, w_o = bf16_w
    b_args = (h, w_rest, w["conv_b_w"], w["conv_b_b"], w["ln_b_gain"], w["ln_b_bias"], w_b, 0)
    if state_b is None:
        yb, b_tail = _branch_b_prompt(*b_args, tm=t["b"], cw=256, seq_len=seq_len)
    else:
        yb, b_tail = _branch_b_sample(*b_args, tm=t["b"], cw=256, state=state_b)
    mix = _merge(h, ya, yb, w_rest, w_out_a, w_out_b, 3 * w_b, tm=t["merge"], cn=512)
    y = _out_proj(mix, x, w_o, w["final_norm_gain"], tm=t["out"], cn=512)
    return y, a_tail, b_tail, bf16_w


def kernel(x_prompt, x_sample, state_conv_a, state_conv_b, norm_gain, w_in, conv_a_w, conv_a_b, w_out_a,
           conv_b_w, conv_b_b, ln_b_gain, ln_b_bias, w_out_b, w_o, final_norm_gain):
    depth = norm_gain.shape[0]
    assert depth == 1, "the output projection kernel applies the final norm: one layer only"
    batch, seq, d = x_prompt.shape
    dec_batch, dec_seq, _ = x_sample.shape
    assert dec_seq == 1
    w_a = conv_a_w.shape[2]
    w = dict(norm_gain=norm_gain[0], w_in=w_in[0], w_in_a=w_in[0, :, :4 * w_a].astype(BF16),
             conv_a_w=conv_a_w[0], conv_a_b=conv_a_b[0], w_out_a=w_out_a[0], conv_b_w=conv_b_w[0],
             conv_b_b=conv_b_b[0], ln_b_gain=ln_b_gain[0], ln_b_bias=ln_b_bias[0], w_out_b=w_out_b[0],
             w_o=w_o[0], final_norm_gain=final_norm_gain)

    yp, a_tail, b_tail, bf16_w = _layer(x_prompt.reshape(batch * seq, d), w, seq_len=seq)
    new_a_prompt = a_tail[:, HIST_A - (CONV_A - 1):, :]
    new_b_prompt = b_tail[:, HIST_B - (CONV_B - 1):, :]

    sa, sb = state_conv_a[0], state_conv_b[0]
    ys, cv_s, u_s, _ = _layer(x_sample.reshape(dec_batch, d), w, bf16_w, state_a=sa, state_b=sb)
    new_a_sample = jnp.concatenate([sa[:, 1:, :], cv_s[:, None, :]], axis=1)
    new_b_sample = jnp.concatenate([sb[:, 1:, :], u_s[:, None, :]], axis=1)

    return (yp.reshape(batch, seq, d), ys.reshape(dec_batch, dec_seq, d),
            new_a_prompt[None], new_b_prompt[None], new_a_sample[None], new_b_sample[None])
```

```python
import functools

import jax
import jax.numpy as jnp
from jax import lax
from jax.experimental import pallas as pl
from jax.experimental.pallas import tpu as pltpu

EPS = 1e-6
CONV_A = 3
CONV_B = 31
SUBLANES = 8
HIST_A = SUBLANES
HIST_B = 4 * SUBLANES
VMEM_LIMIT = 60 * 1024 * 1024

F32 = jnp.float32
BF16 = jnp.bfloat16


def _params(n_axes):
    return pltpu.CompilerParams(
        dimension_semantics=("arbitrary",) * n_axes, vmem_limit_bytes=VMEM_LIMIT)


def _dot(a, w_ref):
    return jnp.dot(a, w_ref[...], preferred_element_type=F32)


def _silu(x):
    return x * jax.nn.sigmoid(x)


def _rmsnorm_kernel(x_ref, g_ref, h_ref):
    x = x_ref[...]
    ms = jnp.mean(x * x, axis=-1, keepdims=True)
    h_ref[...] = (x * lax.rsqrt(ms + EPS) * g_ref[...]).astype(h_ref.dtype)


def _rmsnorm_cast(x, gain, tm):
    m, d = x.shape
    return pl.pallas_call(
        _rmsnorm_kernel,
        grid=(m // tm,),
        in_specs=[pl.BlockSpec((tm, d), lambda i: (i, 0)),
                  pl.BlockSpec((1, d), lambda i: (0, 0))],
        out_specs=pl.BlockSpec((tm, d), lambda i: (i, 0)),
        out_shape=jax.ShapeDtypeStruct((m, d), BF16),
        compiler_params=_params(1),
    )(x, gain.reshape(1, d))


def _cast_slab(t, n_steps, jobs, srcs, dsts, inbufs, outbufs, in_sems, out_sems):
    def read(i, step, slot):
        rows, col0, cols = jobs[i]
        src = srcs[i].at[pl.ds(pl.multiple_of(step * rows, rows), rows), pl.ds(col0, cols)]
        return pltpu.make_async_copy(src, inbufs[i].at[slot], in_sems.at[slot, i])

    def write(i, step):
        rows = jobs[i][0]
        dst = dsts[i].at[pl.ds(pl.multiple_of(step * rows, rows), rows), :]
        return pltpu.make_async_copy(outbufs[i], dst, out_sems.at[i])

    n = len(jobs)
    slot = t % 2

    @pl.when(t == 0)
    def _():
        for i in range(n):
            read(i, 0, 0).start()

    for i in range(n):
        read(i, t, slot).wait()

    @pl.when(t + 1 < n_steps)
    def _():
        for i in range(n):
            read(i, t + 1, 1 - slot).start()

    @pl.when(t > 0)
    def _():
        for i in range(n):
            write(i, t - 1).wait()

    for i in range(n):
        outbufs[i][...] = inbufs[i][slot].astype(outbufs[i].dtype)
    for i in range(n):
        write(i, t).start()

    @pl.when(t == n_steps - 1)
    def _():
        for i in range(n):
            write(i, t).wait()


def _branch_a_prompt_kernel(h_ref, wb_ref, wc_ref, wv_ref, wz_ref, cw_ref, cb_ref, *rest,
                            tiles_per_seq, jobs):
    n = len(jobs)
    srcs, rest = rest[:n], rest[n:]
    ya_ref, cvlast_ref = rest[:2]
    dsts, rest = rest[2:2 + n], rest[2 + n:]
    buf_ref, carry_ref = rest[:2]
    inbufs, outbufs = rest[2:2 + n], rest[2 + n:2 + 2 * n]
    in_sems, out_sems = rest[2 + 2 * n:]
    m, j = pl.program_id(0), pl.program_id(1)
    n_steps = pl.num_programs(0) * pl.num_programs(1)
    _cast_slab(m * pl.num_programs(1) + j, n_steps, jobs, srcs, dsts, inbufs, outbufs, in_sems, out_sems)
    tm = h_ref.shape[0]
    h = h_ref[...]
    cv = _dot(h, wc_ref) * _dot(h, wv_ref)

    @pl.when(m % tiles_per_seq == 0)
    def _():
        buf_ref[0:HIST_A, :] = jnp.zeros((HIST_A, cv.shape[1]), F32)

    @pl.when(m % tiles_per_seq != 0)
    def _():
        buf_ref[0:HIST_A, :] = carry_ref[j]

    buf_ref[HIST_A:, :] = cv
    last = cv[tm - HIST_A:, :]
    carry_ref[j] = last
    cvlast_ref[0] = last
    conv = (cw_ref[0:1, :] * buf_ref[HIST_A - 2:HIST_A - 2 + tm, :]
            + cw_ref[1:2, :] * buf_ref[HIST_A - 1:HIST_A - 1 + tm, :]
            + cw_ref[2:3, :] * cv + cb_ref[...])
    ya = _dot(h, wb_ref) * conv * _silu(_dot(h, wz_ref))
    ya_ref[...] = ya.astype(ya_ref.dtype)


def _branch_a_sample_kernel(h_ref, wb_ref, wc_ref, wv_ref, wz_ref, cw_ref, cb_ref, s0_ref, s1_ref,
                            ya_ref, cv_ref):
    h = h_ref[...]
    cv = _dot(h, wc_ref) * _dot(h, wv_ref)
    cv_ref[...] = cv
    conv = (cw_ref[0:1, :] * s0_ref[...] + cw_ref[1:2, :] * s1_ref[...]
            + cw_ref[2:3, :] * cv + cb_ref[...])
    ya = _dot(h, wb_ref) * conv * _silu(_dot(h, wz_ref))
    ya_ref[...] = ya.astype(ya_ref.dtype)


def _branch_a(h, w_in, conv_w, conv_b, w_a, *, tm, cw, seq_len=None, state=None, casts=()):
    m, d = h.shape
    n_chunks = w_a // cw
    col = lambda k: (lambda i, j: (0, k * n_chunks + j))
    in_specs = [pl.BlockSpec((tm, d), lambda i, j: (i, 0))]
    in_specs += [pl.BlockSpec((d, cw), col(k)) for k in (0, 1, 2, 3)]
    in_specs += [pl.BlockSpec((CONV_A, cw), lambda i, j: (0, j)),
                 pl.BlockSpec((1, cw), lambda i, j: (0, j))]
    args = [h, w_in, w_in, w_in, w_in, conv_w, conv_b.reshape(1, w_a)]
    ya_spec = pl.BlockSpec((tm, cw), lambda i, j: (i, j))
    ya_shape = jax.ShapeDtypeStruct((m, w_a), BF16)
    if state is None:
        tiles_per_seq = seq_len // tm
        n_steps = (m // tm) * n_chunks
        jobs = tuple((src.shape[0] // n_steps, col0, cols) for src, col0, cols in casts)
        assert all(src.shape[0] == rows * n_steps and rows % 16 == 0
                   for (src, _, _), (rows, _, _) in zip(casts, jobs))
        any_spec = pl.BlockSpec(memory_space=pl.ANY)
        kernel = functools.partial(_branch_a_prompt_kernel, tiles_per_seq=tiles_per_seq, jobs=jobs)
        in_specs += [any_spec] * len(jobs)
        args += [src for src, _, _ in casts]
        out_specs = [ya_spec, pl.BlockSpec((1, HIST_A, cw), lambda i, j: (i, 0, j))] + [any_spec] * len(jobs)
        out_shape = [ya_shape, jax.ShapeDtypeStruct((m // tm, HIST_A, w_a), F32)]
        out_shape += [jax.ShapeDtypeStruct((src.shape[0], cols), BF16) for src, _, cols in casts]
        scratch = [pltpu.VMEM((tm + HIST_A, cw), F32), pltpu.VMEM((n_chunks, HIST_A, cw), F32)]
        scratch += [pltpu.VMEM((2, rows, cols), F32) for rows, _, cols in jobs]
        scratch += [pltpu.VMEM((rows, cols), BF16) for rows, _, cols in jobs]
        scratch += [pltpu.SemaphoreType.DMA((2, len(jobs))), pltpu.SemaphoreType.DMA((len(jobs),))]
    else:
        kernel = _branch_a_sample_kernel
        in_specs += [pl.BlockSpec((tm, cw), lambda i, j: (i, j))] * 2
        args += [state[:, 0, :], state[:, 1, :]]
        out_specs = [ya_spec, pl.BlockSpec((tm, cw), lambda i, j: (i, j))]
        out_shape = [ya_shape, jax.ShapeDtypeStruct((m, w_a), F32)]
        scratch = []
    ya, tail, *cast_out = pl.pallas_call(
        kernel, grid=(m // tm, n_chunks), in_specs=in_specs, out_specs=out_specs,
        out_shape=out_shape, scratch_shapes=scratch, compiler_params=_params(2),
    )(*args)
    if state is None:
        tail = tail[tiles_per_seq - 1::tiles_per_seq]
    return ya, tail, cast_out


LN_ROWS = 16
LN_UNROLL = 8


def _layernorm_gate(conv_ref, sz_ref, lng_ref, lnb_ref, yb_ref):
    tm, w = conv_ref.shape
    g = lng_ref[...]
    b = lnb_ref[...]

    def body(i, _):
        r0 = pl.multiple_of(i * LN_ROWS, LN_ROWS)
        x = conv_ref[pl.ds(r0, LN_ROWS), :]
        mu = jnp.mean(x, axis=-1, keepdims=True)
        xc = x - mu
        var = jnp.mean(xc * xc, axis=-1, keepdims=True)
        r = xc * lax.rsqrt(var + EPS) * g + b
        yb_ref[pl.ds(r0, LN_ROWS), :] = (_silu(r) * sz_ref[pl.ds(r0, LN_ROWS), :]).astype(yb_ref.dtype)
        return 0

    lax.fori_loop(0, tm // LN_ROWS, body, 0, unroll=LN_UNROLL)


CONV_ROWS = 32


MXU_K = 256
ANCHOR_TILE = (16, 128)


def _anchored_weights(w_ref, anchors, never):
    ar, ac = ANCHOR_TILE
    pieces = []
    for i in range(w_ref.shape[0] // MXU_K):
        r0 = i * MXU_K
        if i not in anchors:
            pieces.append(w_ref[r0:r0 + MXU_K, :])
            continue
        first = jnp.where(never, anchors[i].astype(w_ref.dtype), w_ref[r0:r0 + ar, 0:ac])
        top = jnp.concatenate([first, w_ref[r0:r0 + ar, ac:]], axis=1)
        pieces.append(jnp.concatenate([top, w_ref[r0 + ar:r0 + MXU_K, :]], axis=0))
    return jnp.concatenate(pieces, axis=0)


def _fold_to_anchor(x):
    ar, ac = ANCHOR_TILE
    tiles = [x[r:r + ar, c:c + ac] for r in range(0, x.shape[0], ar) for c in range(0, x.shape[1], ac)]
    out = tiles[0]
    for t in tiles[1:]:
        out = out + t
    return out


def _branch_b_prompt_kernel(h_ref, wua_ref, wug_ref, wzb_ref, cw_ref, cb_ref, lng_ref, lnb_ref,
                            yb_ref, ulast_ref, ua_ref, ug_ref, zb_ref, ush_ref, carry_ref, conv_ref, sz_ref,
                            *, tiles_per_seq, n_chunks, n_steps):
    s = pl.program_id(0)
    prev = jnp.maximum(s - 1, 0)
    mp, jp = prev // n_chunks, prev % n_chunks
    tm = h_ref.shape[0]
    cw = cw_ref.shape[1]
    n_rows = tm + HIST_B
    lanes_p = pl.ds(pl.multiple_of(jp * cw, cw), cw)

    @pl.when(s == 0)
    def _():
        for ref in (ua_ref, ug_ref, zb_ref, carry_ref):
            ref[...] = jnp.zeros(ref.shape, F32)

    u = ua_ref[...] * jax.nn.sigmoid(ug_ref[...])
    szv = _silu(zb_ref[...])
    sz_ref[:, lanes_p] = szv
    hist = jnp.where(mp % tiles_per_seq != 0, carry_ref[jp], 0.0)
    tail = u[tm - HIST_B:, :]
    carry_ref[jp] = tail
    ulast_ref[0] = tail
    staged = jnp.concatenate([hist, u], axis=0)
    ush_ref[0] = staged
    for r in range(1, SUBLANES):
        ush_ref[r] = pltpu.roll(staged, n_rows - r, axis=0)

    first = HIST_B - (CONV_B - 1)

    def conv_rows(r0):
        acc = cb_ref[...]
        for k in range(CONV_B):
            r, q = (first + k) % SUBLANES, (first + k) // SUBLANES
            acc = acc + cw_ref[k:k + 1, :] * ush_ref[r, r0 + q * SUBLANES:r0 + q * SUBLANES + CONV_ROWS, :]
        conv_ref[r0:r0 + CONV_ROWS, lanes_p] = acc
        return _fold_to_anchor(acc)

    deps = [conv_rows(r0) for r0 in range(0, tm, CONV_ROWS)]
    deps.append(_fold_to_anchor(szv[:, 0:ANCHOR_TILE[1]]))

    n_slabs = h_ref.shape[1] // MXU_K
    first_slab, last_slab = 6, 3 * n_slabs - 2
    anchors = [{}, {}, {}]
    for i, dep in enumerate(deps):
        slab = first_slab + (last_slab - first_slab) * i // (len(deps) - 1)
        anchors[slab // n_slabs][slab % n_slabs] = dep
    never = s < 0
    h = h_ref[...]
    ua_ref[...] = jnp.dot(h, _anchored_weights(wua_ref, anchors[0], never), preferred_element_type=F32)
    ug_ref[...] = jnp.dot(h, _anchored_weights(wug_ref, anchors[1], never), preferred_element_type=F32)
    zb_ref[...] = jnp.dot(h, _anchored_weights(wzb_ref, anchors[2], never), preferred_element_type=F32)

    @pl.when(jnp.logical_and(s > 0, jp == n_chunks - 1))
    def _():
        _layernorm_gate(conv_ref, sz_ref, lng_ref, lnb_ref, yb_ref)


def _branch_b_prompt(h, w_in, conv_w, conv_b, ln_g, ln_b, w_b, col0, *, tm, cw, seq_len):
    m, d = h.shape
    n_chunks = w_b // cw
    n_tiles = m // tm
    n_steps = n_tiles * n_chunks
    tiles_per_seq = seq_len // tm
    cur = lambda s: jnp.minimum(s, n_steps - 1)
    prev = lambda s: jnp.maximum(s - 1, 0)
    col = lambda k: (lambda s: (0, col0 // cw + k * n_chunks + cur(s) % n_chunks))
    in_specs = [pl.BlockSpec((tm, d), lambda s: (cur(s) // n_chunks, 0))]
    in_specs += [pl.BlockSpec((d, cw), col(k)) for k in (0, 1, 2)]
    in_specs += [pl.BlockSpec((CONV_B, cw), lambda s: (0, prev(s) % n_chunks)),
                 pl.BlockSpec((1, cw), lambda s: (0, prev(s) % n_chunks)),
                 pl.BlockSpec((1, w_b), lambda s: (0, 0)),
                 pl.BlockSpec((1, w_b), lambda s: (0, 0))]
    kernel = functools.partial(_branch_b_prompt_kernel, tiles_per_seq=tiles_per_seq,
                               n_chunks=n_chunks, n_steps=n_steps)
    yb, tail = pl.pallas_call(
        kernel, grid=(n_steps + 1,), in_specs=in_specs,
        out_specs=[pl.BlockSpec((tm, w_b), lambda s: (prev(s) // n_chunks, 0)),
                   pl.BlockSpec((1, HIST_B, cw), lambda s: (prev(s) // n_chunks, 0, prev(s) % n_chunks))],
        out_shape=[jax.ShapeDtypeStruct((m, w_b), BF16),
                   jax.ShapeDtypeStruct((n_tiles, HIST_B, w_b), F32)],
        scratch_shapes=[pltpu.VMEM((tm, cw), F32),
                        pltpu.VMEM((tm, cw), F32),
                        pltpu.VMEM((tm, cw), F32),
                        pltpu.VMEM((SUBLANES, tm + HIST_B, cw), F32),
                        pltpu.VMEM((n_chunks, HIST_B, cw), F32),
                        pltpu.VMEM((tm, w_b), F32),
                        pltpu.VMEM((tm, w_b), F32)],
        compiler_params=_params(1),
    )(h, w_in, w_in, w_in, conv_w, conv_b.reshape(1, w_b), ln_g.reshape(1, w_b), ln_b.reshape(1, w_b))
    return yb, tail[tiles_per_seq - 1::tiles_per_seq]


def _branch_b_sample_kernel(h_ref, wua_ref, wug_ref, wzb_ref, cw_ref, cb_ref, lng_ref, lnb_ref, st_ref,
                            yb_ref, u_ref, conv_ref, sz_ref):
    j = pl.program_id(1)
    cw = cw_ref.shape[1]
    h = h_ref[...]
    u = _dot(h, wua_ref) * jax.nn.sigmoid(_dot(h, wug_ref))
    u_ref[...] = u
    lanes = pl.ds(pl.multiple_of(j * cw, cw), cw)
    sz_ref[:, lanes] = _silu(_dot(h, wzb_ref))
    acc = cb_ref[...] + cw_ref[CONV_B - 1:CONV_B, :] * u
    for k in range(CONV_B - 1):
        acc = acc + cw_ref[k:k + 1, :] * st_ref[k]
    conv_ref[:, lanes] = acc

    @pl.when(j == pl.num_programs(1) - 1)
    def _():
        _layernorm_gate(conv_ref, sz_ref, lng_ref, lnb_ref, yb_ref)


def _branch_b_sample(h, w_in, conv_w, conv_b, ln_g, ln_b, w_b, col0, *, tm, cw, state):
    m, d = h.shape
    n_chunks = w_b // cw
    col = lambda k: (lambda i, j: (0, col0 // cw + k * n_chunks + j))
    in_specs = [pl.BlockSpec((tm, d), lambda i, j: (i, 0))]
    in_specs += [pl.BlockSpec((d, cw), col(k)) for k in (0, 1, 2)]
    in_specs += [pl.BlockSpec((CONV_B, cw), lambda i, j: (0, j)),
                 pl.BlockSpec((1, cw), lambda i, j: (0, j)),
                 pl.BlockSpec((1, w_b), lambda i, j: (0, 0)),
                 pl.BlockSpec((1, w_b), lambda i, j: (0, 0)),
                 pl.BlockSpec((CONV_B - 1, tm, cw), lambda i, j: (0, i, j))]
    return pl.pallas_call(
        _branch_b_sample_kernel, grid=(m // tm, n_chunks), in_specs=in_specs,
        out_specs=[pl.BlockSpec((tm, w_b), lambda i, j: (i, 0)),
                   pl.BlockSpec((tm, cw), lambda i, j: (i, j))],
        out_shape=[jax.ShapeDtypeStruct((m, w_b), BF16), jax.ShapeDtypeStruct((m, w_b), F32)],
        scratch_shapes=[pltpu.VMEM((tm, w_b), F32), pltpu.VMEM((tm, w_b), F32)],
        compiler_params=_params(2),
    )(h, w_in, w_in, w_in, conv_w, conv_b.reshape(1, w_b), ln_g.reshape(1, w_b), ln_b.reshape(1, w_b),
      jnp.transpose(state, (1, 0, 2)))


def _merge_kernel(h_ref, ya_ref, yb_ref, wga_ref, wgb_ref, woa_ref, wob_ref, mix_ref):
    h = h_ref[...]
    mix = (jax.nn.sigmoid(_dot(h, wga_ref)) * _dot(ya_ref[...], woa_ref)
           + jax.nn.sigmoid(_dot(h, wgb_ref)) * _dot(yb_ref[...], wob_ref))
    mix_ref[...] = mix.astype(mix_ref.dtype)


def _merge(h, ya, yb, w_in, w_out_a, w_out_b, col0, *, tm, cn):
    m, d = h.shape
    w = ya.shape[1]
    n_chunks = d // cn
    gcol = lambda k: (lambda i, j: (0, col0 // cn + k * n_chunks + j))
    return pl.pallas_call(
        _merge_kernel,
        grid=(m // tm, n_chunks),
        in_specs=[pl.BlockSpec((tm, d), lambda i, j: (i, 0)),
                  pl.BlockSpec((tm, w), lambda i, j: (i, 0)),
                  pl.BlockSpec((tm, w), lambda i, j: (i, 0)),
                  pl.BlockSpec((d, cn), gcol(0)),
                  pl.BlockSpec((d, cn), gcol(1)),
                  pl.BlockSpec((w, cn), lambda i, j: (0, j)),
                  pl.BlockSpec((w, cn), lambda i, j: (0, j))],
        out_specs=pl.BlockSpec((tm, cn), lambda i, j: (i, j)),
        out_shape=jax.ShapeDtypeStruct((m, d), BF16),
        compiler_params=_params(2),
    )(h, ya, yb, w_in, w_in, w_out_a, w_out_b)


def _out_kernel(mix_ref, x_ref, wo_ref, g_ref, y_ref, ssq_ref):
    j = pl.program_id(1)
    cn = wo_ref.shape[1]
    o = x_ref[...] + _dot(mix_ref[...], wo_ref)
    y_ref[:, pl.ds(pl.multiple_of(j * cn, cn), cn)] = o
    part = jnp.sum(o * o, axis=-1, keepdims=True)

    @pl.when(j == 0)
    def _():
        ssq_ref[...] = part

    @pl.when(j != 0)
    def _():
        ssq_ref[...] += part

    @pl.when(j == pl.num_programs(1) - 1)
    def _():
        scale = lax.rsqrt(ssq_ref[...] / y_ref.shape[1] + EPS)
        y_ref[...] = y_ref[...] * scale * g_ref[...]


def _out_proj(mix, x, w_o, gain, *, tm, cn):
    m, d = x.shape
    return pl.pallas_call(
        _out_kernel,
        grid=(m // tm, d // cn),
        in_specs=[pl.BlockSpec((tm, d), lambda i, j: (i, 0)),
                  pl.BlockSpec((tm, cn), lambda i, j: (i, j)),
                  pl.BlockSpec((d, cn), lambda i, j: (0, j)),
                  pl.BlockSpec((1, d), lambda i, j: (0, 0))],
        out_specs=pl.BlockSpec((tm, d), lambda i, j: (i, 0)),
        out_shape=jax.ShapeDtypeStruct((m, d), F32),
        scratch_shapes=[pltpu.VMEM((tm, 1), F32)],
        compiler_params=_params(2),
    )(mix, x, w_o, gain.reshape(1, d))


def _tiles(m):
    if m >= 1024:
        return dict(norm=512, a=1024, b=512, merge=512, out=512)
    return dict(norm=m, a=m, b=m, merge=m, out=m)


def _layer(x, w, bf16_w=None, *, seq_len=None, state_a=None, state_b=None):
    m, d = x.shape
    w_a = w["conv_a_w"].shape[1]
    w_b = w["conv_b_w"].shape[1]
    t = _tiles(m)
    h = _rmsnorm_cast(x, w["norm_gain"], t["norm"])
    casts = ()
    if bf16_w is None:
        w_in = w["w_in"]
        casts = ((w_in, 4 * w_a, w_in.shape[1] - 4 * w_a), (w["w_out_a"], 0, d), (w["w_out_b"], 0, d),
                 (w["w_o"], 0, d))
    ya, a_tail, cast_out = _branch_a(h, w["w_in_a"], w["conv_a_w"], w["conv_a_b"], w_a,
                                     tm=t["a"], cw=256, seq_len=seq_len, state=state_a, casts=casts)
    if bf16_w is None:
        bf16_w = tuple(cast_out)
    w_rest, w_out_a, w_out_b, w_o = bf16_w
    b_args = (h, w_rest, w["conv_b_w"], w["conv_b_b"], w["ln_b_gain"], w["ln_b_bias"], w_b, 0)
    if state_b is None:
        yb, b_tail = _branch_b_prompt(*b_args, tm=t["b"], cw=256, seq_len=seq_len)
    else:
        yb, b_tail = _branch_b_sample(*b_args, tm=t["b"], cw=256, state=state_b)
    mix = _merge(h, ya, yb, w_rest, w_out_a, w_out_b, 3 * w_b, tm=t["merge"], cn=512)
    y = _out_proj(mix, x, w_o, w["final_norm_gain"], tm=t["out"], cn=1024)
    return y, a_tail, b_tail, bf16_w


def kernel(x_prompt, x_sample, state_conv_a, state_conv_b, norm_gain, w_in, conv_a_w, conv_a_b, w_out_a,
           conv_b_w, conv_b_b, ln_b_gain, ln_b_bias, w_out_b, w_o, final_norm_gain):
    depth = norm_gain.shape[0]
    assert depth == 1, "the output projection kernel applies the final norm: one layer only"
    batch, seq, d = x_prompt.shape
    dec_batch, dec_seq, _ = x_sample.shape
    assert dec_seq == 1
    w_a = conv_a_w.shape[2]
    w = dict(norm_gain=norm_gain[0], w_in=w_in[0], w_in_a=w_in[0, :, :4 * w_a].astype(BF16),
             conv_a_w=conv_a_w[0], conv_a_b=conv_a_b[0], w_out_a=w_out_a[0], conv_b_w=conv_b_w[0],
             conv_b_b=conv_b_b[0], ln_b_gain=ln_b_gain[0], ln_b_bias=ln_b_bias[0], w_out_b=w_out_b[0],
             w_o=w_o[0], final_norm_gain=final_norm_gain)

    yp, a_tail, b_tail, bf16_w = _layer(x_prompt.reshape(batch * seq, d), w, seq_len=seq)
    new_a_prompt = a_tail[:, HIST_A - (CONV_A - 1):, :]
    new_b_prompt = b_tail[:, HIST_B - (CONV_B - 1):, :]

    sa, sb = state_conv_a[0], state_conv_b[0]
    ys, cv_s, u_s, _ = _layer(x_sample.reshape(dec_batch, d), w, bf16_w, state_a=sa, state_b=sb)
    new_a_sample = jnp.concatenate([sa[:, 1:, :], cv_s[:, None, :]], axis=1)
    new_b_sample = jnp.concatenate([sb[:, 1:, :], u_s[:, None, :]], axis=1)

    return (yp.reshape(batch, seq, d), ys.reshape(dec_batch, dec_seq, d),
            new_a_prompt[None], new_b_prompt[None], new_a_sample[None], new_b_sample[None])
```

```python
import functools

import jax
import jax.numpy as jnp
from jax import lax
from jax.experimental import pallas as pl
from jax.experimental.pallas import tpu as pltpu

EPS = 1e-6
CONV_A = 3
CONV_B = 31
SUBLANES = 8
HIST_A = SUBLANES
HIST_B = 4 * SUBLANES
VMEM_LIMIT = 60 * 1024 * 1024

F32 = jnp.float32
BF16 = jnp.bfloat16


def _params(n_axes):
    return pltpu.CompilerParams(
        dimension_semantics=("arbitrary",) * n_axes, vmem_limit_bytes=VMEM_LIMIT)


def _dot(a, w_ref):
    return jnp.dot(a, w_ref[...], preferred_element_type=F32)


def _silu(x):
    return x * jax.nn.sigmoid(x)


def _rmsnorm_kernel(x_ref, g_ref, h_ref):
    x = x_ref[...]
    ms = jnp.mean(x * x, axis=-1, keepdims=True)
    h_ref[...] = (x * lax.rsqrt(ms + EPS) * g_ref[...]).astype(h_ref.dtype)


def _rmsnorm_cast(x, gain, tm):
    m, d = x.shape
    return pl.pallas_call(
        _rmsnorm_kernel,
        grid=(m // tm,),
        in_specs=[pl.BlockSpec((tm, d), lambda i: (i, 0)),
                  pl.BlockSpec((1, d), lambda i: (0, 0))],
        out_specs=pl.BlockSpec((tm, d), lambda i: (i, 0)),
        out_shape=jax.ShapeDtypeStruct((m, d), BF16),
        compiler_params=_params(1),
    )(x, gain.reshape(1, d))


def _cast_slab(t, n_steps, jobs, srcs, dsts, inbufs, outbufs, in_sems, out_sems):
    def read(i, step, slot):
        rows, col0, cols = jobs[i]
        src = srcs[i].at[pl.ds(pl.multiple_of(step * rows, rows), rows), pl.ds(col0, cols)]
        return pltpu.make_async_copy(src, inbufs[i].at[slot], in_sems.at[slot, i])

    def write(i, step):
        rows = jobs[i][0]
        dst = dsts[i].at[pl.ds(pl.multiple_of(step * rows, rows), rows), :]
        return pltpu.make_async_copy(outbufs[i], dst, out_sems.at[i])

    n = len(jobs)
    slot = t % 2

    @pl.when(t == 0)
    def _():
        for i in range(n):
            read(i, 0, 0).start()

    for i in range(n):
        read(i, t, slot).wait()

    @pl.when(t + 1 < n_steps)
    def _():
        for i in range(n):
            read(i, t + 1, 1 - slot).start()

    @pl.when(t > 0)
    def _():
        for i in range(n):
            write(i, t - 1).wait()

    for i in range(n):
        outbufs[i][...] = inbufs[i][slot].astype(outbufs[i].dtype)
    for i in range(n):
        write(i, t).start()

    @pl.when(t == n_steps - 1)
    def _():
        for i in range(n):
            write(i, t).wait()


def _branch_a_prompt_kernel(h_ref, wb_ref, wc_ref, wv_ref, wz_ref, cw_ref, cb_ref, *rest,
                            tiles_per_seq, jobs):
    n = len(jobs)
    srcs, rest = rest[:n], rest[n:]
    ya_ref, cvlast_ref = rest[:2]
    dsts, rest = rest[2:2 + n], rest[2 + n:]
    buf_ref, carry_ref = rest[:2]
    inbufs, outbufs = rest[2:2 + n], rest[2 + n:2 + 2 * n]
    in_sems, out_sems = rest[2 + 2 * n:]
    m, j = pl.program_id(0), pl.program_id(1)
    n_steps = pl.num_programs(0) * pl.num_programs(1)
    _cast_slab(m * pl.num_programs(1) + j, n_steps, jobs, srcs, dsts, inbufs, outbufs, in_sems, out_sems)
    tm = h_ref.shape[0]
    h = h_ref[...]
    cv = _dot(h, wc_ref) * _dot(h, wv_ref)

    @pl.when(m % tiles_per_seq == 0)
    def _():
        buf_ref[0:HIST_A, :] = jnp.zeros((HIST_A, cv.shape[1]), F32)

    @pl.when(m % tiles_per_seq != 0)
    def _():
        buf_ref[0:HIST_A, :] = carry_ref[j]

    buf_ref[HIST_A:, :] = cv
    last = cv[tm - HIST_A:, :]
    carry_ref[j] = last
    cvlast_ref[0] = last
    conv = (cw_ref[0:1, :] * buf_ref[HIST_A - 2:HIST_A - 2 + tm, :]
            + cw_ref[1:2, :] * buf_ref[HIST_A - 1:HIST_A - 1 + tm, :]
            + cw_ref[2:3, :] * cv + cb_ref[...])
    ya = _dot(h, wb_ref) * conv * _silu(_dot(h, wz_ref))
    ya_ref[...] = ya.astype(ya_ref.dtype)


def _branch_a_sample_kernel(h_ref, wb_ref, wc_ref, wv_ref, wz_ref, cw_ref, cb_ref, s0_ref, s1_ref,
                            ya_ref, cv_ref):
    h = h_ref[...]
    cv = _dot(h, wc_ref) * _dot(h, wv_ref)
    cv_ref[...] = cv
    conv = (cw_ref[0:1, :] * s0_ref[...] + cw_ref[1:2, :] * s1_ref[...]
            + cw_ref[2:3, :] * cv + cb_ref[...])
    ya = _dot(h, wb_ref) * conv * _silu(_dot(h, wz_ref))
    ya_ref[...] = ya.astype(ya_ref.dtype)


def _branch_a(h, w_in, conv_w, conv_b, w_a, *, tm, cw, seq_len=None, state=None, casts=()):
    m, d = h.shape
    n_chunks = w_a // cw
    col = lambda k: (lambda i, j: (0, k * n_chunks + j))
    in_specs = [pl.BlockSpec((tm, d), lambda i, j: (i, 0))]
    in_specs += [pl.BlockSpec((d, cw), col(k)) for k in (0, 1, 2, 3)]
    in_specs += [pl.BlockSpec((CONV_A, cw), lambda i, j: (0, j)),
                 pl.BlockSpec((1, cw), lambda i, j: (0, j))]
    args = [h, w_in, w_in, w_in, w_in, conv_w, conv_b.reshape(1, w_a)]
    ya_spec = pl.BlockSpec((tm, cw), lambda i, j: (i, j))
    ya_shape = jax.ShapeDtypeStruct((m, w_a), BF16)
    if state is None:
        tiles_per_seq = seq_len // tm
        n_steps = (m // tm) * n_chunks
        jobs = tuple((src.shape[0] // n_steps, col0, cols) for src, col0, cols in casts)
        assert all(src.shape[0] == rows * n_steps and rows % 16 == 0
                   for (src, _, _), (rows, _, _) in zip(casts, jobs))
        any_spec = pl.BlockSpec(memory_space=pl.ANY)
        kernel = functools.partial(_branch_a_prompt_kernel, tiles_per_seq=tiles_per_seq, jobs=jobs)
        in_specs += [any_spec] * len(jobs)
        args += [src for src, _, _ in casts]
        out_specs = [ya_spec, pl.BlockSpec((1, HIST_A, cw), lambda i, j: (i, 0, j))] + [any_spec] * len(jobs)
        out_shape = [ya_shape, jax.ShapeDtypeStruct((m // tm, HIST_A, w_a), F32)]
        out_shape += [jax.ShapeDtypeStruct((src.shape[0], cols), BF16) for src, _, cols in casts]
        scratch = [pltpu.VMEM((tm + HIST_A, cw), F32), pltpu.VMEM((n_chunks, HIST_A, cw), F32)]
        scratch += [pltpu.VMEM((2, rows, cols), F32) for rows, _, cols in jobs]
        scratch += [pltpu.VMEM((rows, cols), BF16) for rows, _, cols in jobs]
        scratch += [pltpu.SemaphoreType.DMA((2, len(jobs))), pltpu.SemaphoreType.DMA((len(jobs),))]
    else:
        kernel = _branch_a_sample_kernel
        in_specs += [pl.BlockSpec((tm, cw), lambda i, j: (i, j))] * 2
        args += [state[:, 0, :], state[:, 1, :]]
        out_specs = [ya_spec, pl.BlockSpec((tm, cw), lambda i, j: (i, j))]
        out_shape = [ya_shape, jax.ShapeDtypeStruct((m, w_a), F32)]
        scratch = []
    ya, tail, *cast_out = pl.pallas_call(
        kernel, grid=(m // tm, n_chunks), in_specs=in_specs, out_specs=out_specs,
        out_shape=out_shape, scratch_shapes=scratch, compiler_params=_params(2),
    )(*args)
    if state is None:
        tail = tail[tiles_per_seq - 1::tiles_per_seq]
    return ya, tail, cast_out


LN_ROWS = 16
LN_UNROLL = 8


def _layernorm_gate(conv_ref, sz_ref, lng_ref, lnb_ref, yb_ref):
    tm, w = conv_ref.shape
    g = lng_ref[...]
    b = lnb_ref[...]

    def body(i, _):
        r0 = pl.multiple_of(i * LN_ROWS, LN_ROWS)
        x = conv_ref[pl.ds(r0, LN_ROWS), :]
        mu = jnp.mean(x, axis=-1, keepdims=True)
        xc = x - mu
        var = jnp.mean(xc * xc, axis=-1, keepdims=True)
        r = xc * lax.rsqrt(var + EPS) * g + b
        yb_ref[pl.ds(r0, LN_ROWS), :] = (_silu(r) * sz_ref[pl.ds(r0, LN_ROWS), :]).astype(yb_ref.dtype)
        return 0

    lax.fori_loop(0, tm // LN_ROWS, body, 0, unroll=LN_UNROLL)


CONV_ROWS = 32


MXU_K = 256
ANCHOR_TILE = (16, 128)


def _anchored_weights(w_ref, anchors, never):
    ar, ac = ANCHOR_TILE
    pieces = []
    for i in range(w_ref.shape[0] // MXU_K):
        r0 = i * MXU_K
        if i not in anchors:
            pieces.append(w_ref[r0:r0 + MXU_K, :])
            continue
        first = jnp.where(never, anchors[i].astype(w_ref.dtype), w_ref[r0:r0 + ar, 0:ac])
        top = jnp.concatenate([first, w_ref[r0:r0 + ar, ac:]], axis=1)
        pieces.append(jnp.concatenate([top, w_ref[r0 + ar:r0 + MXU_K, :]], axis=0))
    return jnp.concatenate(pieces, axis=0)


def _fold_to_anchor(x):
    ar, ac = ANCHOR_TILE
    tiles = [x[r:r + ar, c:c + ac] for r in range(0, x.shape[0], ar) for c in range(0, x.shape[1], ac)]
    out = tiles[0]
    for t in tiles[1:]:
        out = out + t
    return out


def _branch_b_prompt_kernel(h_ref, wua_ref, wug_ref, wzb_ref, cw_ref, cb_ref, lng_ref, lnb_ref,
                            yb_ref, ulast_ref, ua_ref, ug_ref, zb_ref, ush_ref, carry_ref, conv_ref, sz_ref,
                            *, tiles_per_seq, n_chunks, n_steps):
    s = pl.program_id(0)
    prev = jnp.maximum(s - 1, 0)
    mp, jp = prev // n_chunks, prev % n_chunks
    tm = h_ref.shape[0]
    cw = cw_ref.shape[1]
    n_rows = tm + HIST_B
    lanes_p = pl.ds(pl.multiple_of(jp * cw, cw), cw)

    @pl.when(s == 0)
    def _():
        for ref in (ua_ref, ug_ref, zb_ref, carry_ref):
            ref[...] = jnp.zeros(ref.shape, F32)

    u = ua_ref[...] * jax.nn.sigmoid(ug_ref[...])
    szv = _silu(zb_ref[...])
    sz_ref[:, lanes_p] = szv
    hist = jnp.where(mp % tiles_per_seq != 0, carry_ref[jp], 0.0)
    tail = u[tm - HIST_B:, :]
    carry_ref[jp] = tail
    ulast_ref[0] = tail
    staged = jnp.concatenate([hist, u], axis=0)
    ush_ref[0] = staged
    for r in range(1, SUBLANES):
        ush_ref[r] = pltpu.roll(staged, n_rows - r, axis=0)

    first = HIST_B - (CONV_B - 1)

    def conv_rows(r0):
        acc = cb_ref[...]
        for k in range(CONV_B):
            r, q = (first + k) % SUBLANES, (first + k) // SUBLANES
            acc = acc + cw_ref[k:k + 1, :] * ush_ref[r, r0 + q * SUBLANES:r0 + q * SUBLANES + CONV_ROWS, :]
        conv_ref[r0:r0 + CONV_ROWS, lanes_p] = acc
        return _fold_to_anchor(acc)

    deps = [conv_rows(r0) for r0 in range(0, tm, CONV_ROWS)]
    deps.append(_fold_to_anchor(szv[:, 0:ANCHOR_TILE[1]]))

    n_slabs = h_ref.shape[1] // MXU_K
    first_slab, last_slab = 6, 3 * n_slabs - 2
    anchors = [{}, {}, {}]
    for i, dep in enumerate(deps):
        slab = first_slab + (last_slab - first_slab) * i // (len(deps) - 1)
        anchors[slab // n_slabs][slab % n_slabs] = dep
    never = s < 0
    h = h_ref[...]
    ua_ref[...] = jnp.dot(h, _anchored_weights(wua_ref, anchors[0], never), preferred_element_type=F32)
    ug_ref[...] = jnp.dot(h, _anchored_weights(wug_ref, anchors[1], never), preferred_element_type=F32)
    zb_ref[...] = jnp.dot(h, _anchored_weights(wzb_ref, anchors[2], never), preferred_element_type=F32)

    @pl.when(jnp.logical_and(s > 0, jp == n_chunks - 1))
    def _():
        _layernorm_gate(conv_ref, sz_ref, lng_ref, lnb_ref, yb_ref)


def _branch_b_prompt(h, w_in, conv_w, conv_b, ln_g, ln_b, w_b, col0, *, tm, cw, seq_len):
    m, d = h.shape
    n_chunks = w_b // cw
    n_tiles = m // tm
    n_steps = n_tiles * n_chunks
    tiles_per_seq = seq_len // tm
    cur = lambda s: jnp.minimum(s, n_steps - 1)
    prev = lambda s: jnp.maximum(s - 1, 0)
    col = lambda k: (lambda s: (0, col0 // cw + k * n_chunks + cur(s) % n_chunks))
    in_specs = [pl.BlockSpec((tm, d), lambda s: (cur(s) // n_chunks, 0))]
    in_specs += [pl.BlockSpec((d, cw), col(k)) for k in (0, 1, 2)]
    in_specs += [pl.BlockSpec((CONV_B, cw), lambda s: (0, prev(s) % n_chunks)),
                 pl.BlockSpec((1, cw), lambda s: (0, prev(s) % n_chunks)),
                 pl.BlockSpec((1, w_b), lambda s: (0, 0)),
                 pl.BlockSpec((1, w_b), lambda s: (0, 0))]
    kernel = functools.partial(_branch_b_prompt_kernel, tiles_per_seq=tiles_per_seq,
                               n_chunks=n_chunks, n_steps=n_steps)
    yb, tail = pl.pallas_call(
        kernel, grid=(n_steps + 1,), in_specs=in_specs,
        out_specs=[pl.BlockSpec((tm, w_b), lambda s: (prev(s) // n_chunks, 0)),
                   pl.BlockSpec((1, HIST_B, cw), lambda s: (prev(s) // n_chunks, 0, prev(s) % n_chunks))],
        out_shape=[jax.ShapeDtypeStruct((m, w_b), BF16),
                   jax.ShapeDtypeStruct((n_tiles, HIST_B, w_b), F32)],
        scratch_shapes=[pltpu.VMEM((tm, cw), F32),
                        pltpu.VMEM((tm, cw), F32),
                        pltpu.VMEM((tm, cw), F32),
                        pltpu.VMEM((SUBLANES, tm + HIST_B, cw), F32),
                        pltpu.VMEM((n_chunks, HIST_B, cw), F32),
                        pltpu.VMEM((tm, w_b), F32),
                        pltpu.VMEM((tm, w_b), F32)],
        compiler_params=_params(1),
    )(h, w_in, w_in, w_in, conv_w, conv_b.reshape(1, w_b), ln_g.reshape(1, w_b), ln_b.reshape(1, w_b))
    return yb, tail[tiles_per_seq - 1::tiles_per_seq]


def _branch_b_sample_kernel(h_ref, wua_ref, wug_ref, wzb_ref, cw_ref, cb_ref, lng_ref, lnb_ref, st_ref,
                            yb_ref, u_ref, conv_ref, sz_ref):
    j = pl.program_id(1)
    cw = cw_ref.shape[1]
    h = h_ref[...]
    u = _dot(h, wua_ref) * jax.nn.sigmoid(_dot(h, wug_ref))
    u_ref[...] = u
    lanes = pl.ds(pl.multiple_of(j * cw, cw), cw)
    sz_ref[:, lanes] = _silu(_dot(h, wzb_ref))
    acc = cb_ref[...] + cw_ref[CONV_B - 1:CONV_B, :] * u
    for k in range(CONV_B - 1):
        acc = acc + cw_ref[k:k + 1, :] * st_ref[k]
    conv_ref[:, lanes] = acc

    @pl.when(j == pl.num_programs(1) - 1)
    def _():
        _layernorm_gate(conv_ref, sz_ref, lng_ref, lnb_ref, yb_ref)


def _branch_b_sample(h, w_in, conv_w, conv_b, ln_g, ln_b, w_b, col0, *, tm, cw, state):
    m, d = h.shape
    n_chunks = w_b // cw
    col = lambda k: (lambda i, j: (0, col0 // cw + k * n_chunks + j))
    in_specs = [pl.BlockSpec((tm, d), lambda i, j: (i, 0))]
    in_specs += [pl.BlockSpec((d, cw), col(k)) for k in (0, 1, 2)]
    in_specs += [pl.BlockSpec((CONV_B, cw), lambda i, j: (0, j)),
                 pl.BlockSpec((1, cw), lambda i, j: (0, j)),
                 pl.BlockSpec((1, w_b), lambda i, j: (0, 0)),
                 pl.BlockSpec((1, w_b), lambda i, j: (0, 0)),
                 pl.BlockSpec((CONV_B - 1, tm, cw), lambda i, j: (0, i, j))]
    return pl.pallas_call(
        _branch_b_sample_kernel, grid=(m // tm, n_chunks), in_specs=in_specs,
        out_specs=[pl.BlockSpec((tm, w_b), lambda i, j: (i, 0)),
                   pl.BlockSpec((tm, cw), lambda i, j: (i, j))],
        out_shape=[jax.ShapeDtypeStruct((m, w_b), BF16), jax.ShapeDtypeStruct((m, w_b), F32)],
        scratch_shapes=[pltpu.VMEM((tm, w_b), F32), pltpu.VMEM((tm, w_b), F32)],
        compiler_params=_params(2),
    )(h, w_in, w_in, w_in, conv_w, conv_b.reshape(1, w_b), ln_g.reshape(1, w_b), ln_b.reshape(1, w_b),
      jnp.transpose(state, (1, 0, 2)))


def _merge_kernel(h_ref, ya_ref, yb_ref, hx_ref, yax_ref, ybx_ref, wga_ref, wgb_ref, woa_ref, wob_ref,
                  mix_ref, mixx_ref):
    m, last = pl.program_id(0), pl.num_programs(0) - 1
    tm = h_ref.shape[0]

    def mix_of(h, ya, yb):
        return (jax.nn.sigmoid(_dot(h, wga_ref)) * _dot(ya, woa_ref)
                + jax.nn.sigmoid(_dot(h, wgb_ref)) * _dot(yb, wob_ref))

    @pl.when(m != last)
    def _():
        mix_ref[...] = mix_of(h_ref[...], ya_ref[...], yb_ref[...]).astype(mix_ref.dtype)

    @pl.when(m == last)
    def _():
        rows = lambda a, b: jnp.concatenate([a[...], b[...]], axis=0)
        mix = mix_of(rows(h_ref, hx_ref), rows(ya_ref, yax_ref), rows(yb_ref, ybx_ref))
        mix_ref[...] = mix[:tm].astype(mix_ref.dtype)
        mixx_ref[...] = mix[tm:].astype(mixx_ref.dtype)


def _merge(h, ya, yb, extra, w_in, w_out_a, w_out_b, col0, *, tm, cn):
    m, d = h.shape
    mx = extra[0].shape[0]
    w = ya.shape[1]
    n_chunks = d // cn
    n_tiles = m // tm
    gcol = lambda k: (lambda i, j: (0, col0 // cn + k * n_chunks + j))
    whole = lambda i, j: (0, 0)
    return pl.pallas_call(
        _merge_kernel,
        grid=(n_tiles, n_chunks),
        in_specs=[pl.BlockSpec((tm, d), lambda i, j: (i, 0)),
                  pl.BlockSpec((tm, w), lambda i, j: (i, 0)),
                  pl.BlockSpec((tm, w), lambda i, j: (i, 0)),
                  pl.BlockSpec((mx, d), whole),
                  pl.BlockSpec((mx, w), whole),
                  pl.BlockSpec((mx, w), whole),
                  pl.BlockSpec((d, cn), gcol(0)),
                  pl.BlockSpec((d, cn), gcol(1)),
                  pl.BlockSpec((w, cn), lambda i, j: (0, j)),
                  pl.BlockSpec((w, cn), lambda i, j: (0, j))],
        out_specs=[pl.BlockSpec((tm, cn), lambda i, j: (i, j)),
                   pl.BlockSpec((mx, cn), lambda i, j: (0, jnp.where(i == n_tiles - 1, j, 0)))],
        out_shape=[jax.ShapeDtypeStruct((m, d), BF16), jax.ShapeDtypeStruct((mx, d), BF16)],
        compiler_params=_params(2),
    )(h, ya, yb, *extra, w_in, w_in, w_out_a, w_out_b)


def _out_kernel(mix_ref, x_ref, wo_ref, g_ref, y_ref, ssq_ref):
    j = pl.program_id(1)
    cn = wo_ref.shape[1]
    o = x_ref[...] + _dot(mix_ref[...], wo_ref)
    y_ref[:, pl.ds(pl.multiple_of(j * cn, cn), cn)] = o
    part = jnp.sum(o * o, axis=-1, keepdims=True)

    @pl.when(j == 0)
    def _():
        ssq_ref[...] = part

    @pl.when(j != 0)
    def _():
        ssq_ref[...] += part

    @pl.when(j == pl.num_programs(1) - 1)
    def _():
        scale = lax.rsqrt(ssq_ref[...] / y_ref.shape[1] + EPS)
        y_ref[...] = y_ref[...] * scale * g_ref[...]


def _out_proj(mix, x, w_o, gain, *, tm, cn):
    m, d = x.shape
    return pl.pallas_call(
        _out_kernel,
        grid=(m // tm, d // cn),
        in_specs=[pl.BlockSpec((tm, d), lambda i, j: (i, 0)),
                  pl.BlockSpec((tm, cn), lambda i, j: (i, j)),
                  pl.BlockSpec((d, cn), lambda i, j: (0, j)),
                  pl.BlockSpec((1, d), lambda i, j: (0, 0))],
        out_specs=pl.BlockSpec((tm, d), lambda i, j: (i, 0)),
        out_shape=jax.ShapeDtypeStruct((m, d), F32),
        scratch_shapes=[pltpu.VMEM((tm, 1), F32)],
        compiler_params=_params(2),
    )(mix, x, w_o, gain.reshape(1, d))


def _tiles(m):
    if m >= 1024:
        return dict(norm=512, a=1024, b=512, merge=512, out=512, a_cw=256, b_cw=256, merge_cn=512)
    return dict(norm=m, a=m, b=m, merge=m, out=m, a_cw=256, b_cw=256, merge_cn=512)


def _branches(x, w, bf16_w=None, *, seq_len=None, state_a=None, state_b=None):
    m, d = x.shape
    w_a = w["conv_a_w"].shape[1]
    w_b = w["conv_b_w"].shape[1]
    t = _tiles(m)
    h = _rmsnorm_cast(x, w["norm_gain"], t["norm"])
    casts = ()
    if bf16_w is None:
        w_in = w["w_in"]
        casts = ((w_in, 4 * w_a, w_in.shape[1] - 4 * w_a), (w["w_out_a"], 0, d), (w["w_out_b"], 0, d),
                 (w["w_o"], 0, d))
    ya, a_tail, cast_out = _branch_a(h, w["w_in_a"], w["conv_a_w"], w["conv_a_b"], w_a,
                                     tm=t["a"], cw=t["a_cw"], seq_len=seq_len, state=state_a, casts=casts)
    if bf16_w is None:
        bf16_w = tuple(cast_out)
    w_rest, w_out_a, w_out_b, w_o = bf16_w
    b_args = (h, w_rest, w["conv_b_w"], w["conv_b_b"], w["ln_b_gain"], w["ln_b_bias"], w_b, 0)
    if state_b is None:
        yb, b_tail = _branch_b_prompt(*b_args, tm=t["b"], cw=t["b_cw"], seq_len=seq_len)
    else:
        yb, b_tail = _branch_b_sample(*b_args, tm=t["b"], cw=t["b_cw"], state=state_b)
    return (h, ya, yb), a_tail, b_tail, bf16_w


def kernel(x_prompt, x_sample, state_conv_a, state_conv_b, norm_gain, w_in, conv_a_w, conv_a_b, w_out_a,
           conv_b_w, conv_b_b, ln_b_gain, ln_b_bias, w_out_b, w_o, final_norm_gain):
    depth = norm_gain.shape[0]
    assert depth == 1, "the output projection kernel applies the final norm: one layer only"
    batch, seq, d = x_prompt.shape
    dec_batch, dec_seq, _ = x_sample.shape
    assert dec_seq == 1
    w_a = conv_a_w.shape[2]
    w = dict(norm_gain=norm_gain[0], w_in=w_in[0], w_in_a=w_in[0, :, :4 * w_a].astype(BF16),
             conv_a_w=conv_a_w[0], conv_a_b=conv_a_b[0], w_out_a=w_out_a[0], conv_b_w=conv_b_w[0],
             conv_b_b=conv_b_b[0], ln_b_gain=ln_b_gain[0], ln_b_bias=ln_b_bias[0], w_out_b=w_out_b[0],
             w_o=w_o[0], final_norm_gain=final_norm_gain)

    xp, xs = x_prompt.reshape(batch * seq, d), x_sample.reshape(dec_batch, d)
    acts_p, a_tail, b_tail, bf16_w = _branches(xp, w, seq_len=seq)
    new_a_prompt = a_tail[:, HIST_A - (CONV_A - 1):, :]
    new_b_prompt = b_tail[:, HIST_B - (CONV_B - 1):, :]

    sa, sb = state_conv_a[0], state_conv_b[0]
    acts_s, cv_s, u_s, _ = _branches(xs, w, bf16_w, state_a=sa, state_b=sb)
    new_a_sample = jnp.concatenate([sa[:, 1:, :], cv_s[:, None, :]], axis=1)
    new_b_sample = jnp.concatenate([sb[:, 1:, :], u_s[:, None, :]], axis=1)

    w_rest, w_out_a_bf, w_out_b_bf, w_o_bf = bf16_w
    tp, ts = _tiles(xp.shape[0]), _tiles(xs.shape[0])
    mix_p, mix_s = _merge(*acts_p, acts_s, w_rest, w_out_a_bf, w_out_b_bf, 3 * conv_b_w.shape[2],
                          tm=tp["merge"], cn=tp["merge_cn"])
    yp = _out_proj(mix_p, xp, w_o_bf, final_norm_gain, tm=tp["out"], cn=1024)
    ys = _out_proj(mix_s, xs, w_o_bf, final_norm_gain, tm=ts["out"], cn=1024)

    return (yp.reshape(batch, seq, d), ys.reshape(dec_batch, dec_seq, d),
            new_a_prompt[None], new_b_prompt[None], new_a_sample[None], new_b_sample[None])
```

```python
import functools

import jax
import jax.numpy as jnp
from jax import lax
from jax.experimental import pallas as pl
from jax.experimental.pallas import tpu as pltpu

EPS = 1e-6
CONV_A = 3
CONV_B = 31
SUBLANES = 8
HIST_A = SUBLANES
HIST_B = 4 * SUBLANES
VMEM_LIMIT = 60 * 1024 * 1024

F32 = jnp.float32
BF16 = jnp.bfloat16


def _params(n_axes):
    return pltpu.CompilerParams(
        dimension_semantics=("arbitrary",) * n_axes, vmem_limit_bytes=VMEM_LIMIT)


def _dot(a, w_ref):
    return jnp.dot(a, w_ref[...], preferred_element_type=F32)


def _silu(x):
    return x * jax.nn.sigmoid(x)


def _rmsnorm_kernel(x_ref, g_ref, h_ref):
    x = x_ref[...]
    ms = jnp.mean(x * x, axis=-1, keepdims=True)
    h_ref[...] = (x * lax.rsqrt(ms + EPS) * g_ref[...]).astype(h_ref.dtype)


def _rmsnorm_cast(x, gain, tm):
    m, d = x.shape
    return pl.pallas_call(
        _rmsnorm_kernel,
        grid=(m // tm,),
        in_specs=[pl.BlockSpec((tm, d), lambda i: (i, 0)),
                  pl.BlockSpec((1, d), lambda i: (0, 0))],
        out_specs=pl.BlockSpec((tm, d), lambda i: (i, 0)),
        out_shape=jax.ShapeDtypeStruct((m, d), BF16),
        compiler_params=_params(1),
    )(x, gain.reshape(1, d))


def _cast_slab(t, n_steps, jobs, srcs, dsts, inbufs, outbufs, in_sems, out_sems):
    def read(i, step, slot):
        rows, col0, cols = jobs[i]
        src = srcs[i].at[pl.ds(pl.multiple_of(step * rows, rows), rows), pl.ds(col0, cols)]
        return pltpu.make_async_copy(src, inbufs[i].at[slot], in_sems.at[slot, i])

    def write(i, step):
        rows = jobs[i][0]
        dst = dsts[i].at[pl.ds(pl.multiple_of(step * rows, rows), rows), :]
        return pltpu.make_async_copy(outbufs[i], dst, out_sems.at[i])

    n = len(jobs)
    slot = t % 2

    @pl.when(t == 0)
    def _():
        for i in range(n):
            read(i, 0, 0).start()

    for i in range(n):
        read(i, t, slot).wait()

    @pl.when(t + 1 < n_steps)
    def _():
        for i in range(n):
            read(i, t + 1, 1 - slot).start()

    @pl.when(t > 0)
    def _():
        for i in range(n):
            write(i, t - 1).wait()

    for i in range(n):
        outbufs[i][...] = inbufs[i][slot].astype(outbufs[i].dtype)
    for i in range(n):
        write(i, t).start()

    @pl.when(t == n_steps - 1)
    def _():
        for i in range(n):
            write(i, t).wait()


def _branch_a_prompt_kernel(h_ref, wb_ref, wc_ref, wv_ref, wz_ref, cw_ref, cb_ref, *rest,
                            tiles_per_seq, jobs):
    n = len(jobs)
    srcs, rest = rest[:n], rest[n:]
    ya_ref, cvlast_ref = rest[:2]
    dsts, rest = rest[2:2 + n], rest[2 + n:]
    buf_ref, carry_ref = rest[:2]
    inbufs, outbufs = rest[2:2 + n], rest[2 + n:2 + 2 * n]
    in_sems, out_sems = rest[2 + 2 * n:]
    m, j = pl.program_id(0), pl.program_id(1)
    n_steps = pl.num_programs(0) * pl.num_programs(1)
    _cast_slab(m * pl.num_programs(1) + j, n_steps, jobs, srcs, dsts, inbufs, outbufs, in_sems, out_sems)
    tm = h_ref.shape[0]
    h = h_ref[...]
    cv = _dot(h, wc_ref) * _dot(h, wv_ref)

    @pl.when(m % tiles_per_seq == 0)
    def _():
        buf_ref[0:HIST_A, :] = jnp.zeros((HIST_A, cv.shape[1]), F32)

    @pl.when(m % tiles_per_seq != 0)
    def _():
        buf_ref[0:HIST_A, :] = carry_ref[j]

    buf_ref[HIST_A:, :] = cv
    last = cv[tm - HIST_A:, :]
    carry_ref[j] = last
    cvlast_ref[0] = last
    never = m < 0
    rows = tm // A_BLOCKS
    convs, anchors = [], ({}, {})
    for i in range(A_BLOCKS):
        r0 = i * rows
        conv = (cw_ref[0:1, :] * buf_ref[r0 + HIST_A - 2:r0 + HIST_A - 2 + rows, :]
                + cw_ref[1:2, :] * buf_ref[r0 + HIST_A - 1:r0 + HIST_A - 1 + rows, :]
                + cw_ref[2:3, :] * cv[r0:r0 + rows, :] + cb_ref[...])
        convs.append(conv)
        anchors[i % 2][2 + 3 * (i // 2)] = _fold_to_anchor(conv)
    z = jnp.dot(h, _anchored_weights(wz_ref, anchors[0], never), preferred_element_type=F32)
    b = jnp.dot(h, _anchored_weights(wb_ref, anchors[1], never), preferred_element_type=F32)
    for i in range(A_BLOCKS):
        r0 = i * rows
        ya = b[r0:r0 + rows, :] * convs[i] * _silu(z[r0:r0 + rows, :])
        ya_ref[r0:r0 + rows, :] = ya.astype(ya_ref.dtype)


A_BLOCKS = 8


def _branch_a_sample_kernel(h_ref, wb_ref, wc_ref, wv_ref, wz_ref, cw_ref, cb_ref, s0_ref, s1_ref,
                            ya_ref, cv_ref):
    h = h_ref[...]
    cv = _dot(h, wc_ref) * _dot(h, wv_ref)
    cv_ref[...] = cv
    conv = (cw_ref[0:1, :] * s0_ref[...] + cw_ref[1:2, :] * s1_ref[...]
            + cw_ref[2:3, :] * cv + cb_ref[...])
    ya = _dot(h, wb_ref) * conv * _silu(_dot(h, wz_ref))
    ya_ref[...] = ya.astype(ya_ref.dtype)


def _branch_a(h, w_in, conv_w, conv_b, w_a, *, tm, cw, seq_len=None, state=None, casts=()):
    m, d = h.shape
    n_chunks = w_a // cw
    col = lambda k: (lambda i, j: (0, k * n_chunks + j))
    in_specs = [pl.BlockSpec((tm, d), lambda i, j: (i, 0))]
    in_specs += [pl.BlockSpec((d, cw), col(k)) for k in (0, 1, 2, 3)]
    in_specs += [pl.BlockSpec((CONV_A, cw), lambda i, j: (0, j)),
                 pl.BlockSpec((1, cw), lambda i, j: (0, j))]
    args = [h, w_in, w_in, w_in, w_in, conv_w, conv_b.reshape(1, w_a)]
    ya_spec = pl.BlockSpec((tm, cw), lambda i, j: (i, j))
    ya_shape = jax.ShapeDtypeStruct((m, w_a), BF16)
    if state is None:
        tiles_per_seq = seq_len // tm
        n_steps = (m // tm) * n_chunks
        jobs = tuple((src.shape[0] // n_steps, col0, cols) for src, col0, cols in casts)
        assert all(src.shape[0] == rows * n_steps and rows % 16 == 0
                   for (src, _, _), (rows, _, _) in zip(casts, jobs))
        any_spec = pl.BlockSpec(memory_space=pl.ANY)
        kernel = functools.partial(_branch_a_prompt_kernel, tiles_per_seq=tiles_per_seq, jobs=jobs)
        in_specs += [any_spec] * len(jobs)
        args += [src for src, _, _ in casts]
        out_specs = [ya_spec, pl.BlockSpec((1, HIST_A, cw), lambda i, j: (i, 0, j))] + [any_spec] * len(jobs)
        out_shape = [ya_shape, jax.ShapeDtypeStruct((m // tm, HIST_A, w_a), F32)]
        out_shape += [jax.ShapeDtypeStruct((src.shape[0], cols), BF16) for src, _, cols in casts]
        scratch = [pltpu.VMEM((tm + HIST_A, cw), F32), pltpu.VMEM((n_chunks, HIST_A, cw), F32)]
        scratch += [pltpu.VMEM((2, rows, cols), F32) for rows, _, cols in jobs]
        scratch += [pltpu.VMEM((rows, cols), BF16) for rows, _, cols in jobs]
        scratch += [pltpu.SemaphoreType.DMA((2, len(jobs))), pltpu.SemaphoreType.DMA((len(jobs),))]
    else:
        kernel = _branch_a_sample_kernel
        in_specs += [pl.BlockSpec((tm, cw), lambda i, j: (i, j))] * 2
        args += [state[:, 0, :], state[:, 1, :]]
        out_specs = [ya_spec, pl.BlockSpec((tm, cw), lambda i, j: (i, j))]
        out_shape = [ya_shape, jax.ShapeDtypeStruct((m, w_a), F32)]
        scratch = []
    ya, tail, *cast_out = pl.pallas_call(
        kernel, grid=(m // tm, n_chunks), in_specs=in_specs, out_specs=out_specs,
        out_shape=out_shape, scratch_shapes=scratch, compiler_params=_params(2),
    )(*args)
    if state is None:
        tail = tail[tiles_per_seq - 1::tiles_per_seq]
    return ya, tail, cast_out


LN_ROWS = 16
LN_UNROLL = 8


def _layernorm_gate(conv_ref, sz_ref, lng_ref, lnb_ref, yb_ref):
    tm, w = conv_ref.shape
    g = lng_ref[...]
    b = lnb_ref[...]

    def body(i, _):
        r0 = pl.multiple_of(i * LN_ROWS, LN_ROWS)
        x = conv_ref[pl.ds(r0, LN_ROWS), :]
        mu = jnp.mean(x, axis=-1, keepdims=True)
        xc = x - mu
        var = jnp.mean(xc * xc, axis=-1, keepdims=True)
        r = xc * lax.rsqrt(var + EPS) * g + b
        yb_ref[pl.ds(r0, LN_ROWS), :] = (_silu(r) * sz_ref[pl.ds(r0, LN_ROWS), :]).astype(yb_ref.dtype)
        return 0

    lax.fori_loop(0, tm // LN_ROWS, body, 0, unroll=LN_UNROLL)


CONV_ROWS = 32


MXU_K = 256
ANCHOR_TILE = (16, 128)


def _anchored_weights(w_ref, anchors, never):
    ar, ac = ANCHOR_TILE
    pieces = []
    for i in range(w_ref.shape[0] // MXU_K):
        r0 = i * MXU_K
        if i not in anchors:
            pieces.append(w_ref[r0:r0 + MXU_K, :])
            continue
        first = jnp.where(never, anchors[i].astype(w_ref.dtype), w_ref[r0:r0 + ar, 0:ac])
        top = jnp.concatenate([first, w_ref[r0:r0 + ar, ac:]], axis=1)
        pieces.append(jnp.concatenate([top, w_ref[r0 + ar:r0 + MXU_K, :]], axis=0))
    return jnp.concatenate(pieces, axis=0)


def _fold_to_anchor(x):
    ar, ac = ANCHOR_TILE
    tiles = [x[r:r + ar, c:c + ac] for r in range(0, x.shape[0], ar) for c in range(0, x.shape[1], ac)]
    out = tiles[0]
    for t in tiles[1:]:
        out = out + t
    return out


def _branch_b_prompt_kernel(h_ref, wua_ref, wug_ref, wzb_ref, cw_ref, cb_ref, lng_ref, lnb_ref,
                            yb_ref, ulast_ref, ua_ref, ug_ref, zb_ref, ush_ref, carry_ref, conv_ref, sz_ref,
                            *, tiles_per_seq, n_chunks, n_steps):
    s = pl.program_id(0)
    prev = jnp.maximum(s - 1, 0)
    mp, jp = prev // n_chunks, prev % n_chunks
    tm = h_ref.shape[0]
    cw = cw_ref.shape[1]
    n_rows = tm + HIST_B
    lanes_p = pl.ds(pl.multiple_of(jp * cw, cw), cw)

    @pl.when(s == 0)
    def _():
        for ref in (ua_ref, ug_ref, zb_ref, carry_ref):
            ref[...] = jnp.zeros(ref.shape, F32)

    u = ua_ref[...] * jax.nn.sigmoid(ug_ref[...])
    szv = _silu(zb_ref[...])
    sz_ref[:, lanes_p] = szv
    hist = jnp.where(mp % tiles_per_seq != 0, carry_ref[jp], 0.0)
    tail = u[tm - HIST_B:, :]
    carry_ref[jp] = tail
    ulast_ref[0] = tail
    staged = jnp.concatenate([hist, u], axis=0)
    ush_ref[0] = staged
    for r in range(1, SUBLANES):
        ush_ref[r] = pltpu.roll(staged, n_rows - r, axis=0)

    first = HIST_B - (CONV_B - 1)

    def conv_rows(r0):
        acc = cb_ref[...]
        for k in range(CONV_B):
            r, q = (first + k) % SUBLANES, (first + k) // SUBLANES
            acc = acc + cw_ref[k:k + 1, :] * ush_ref[r, r0 + q * SUBLANES:r0 + q * SUBLANES + CONV_ROWS, :]
        conv_ref[r0:r0 + CONV_ROWS, lanes_p] = acc
        return _fold_to_anchor(acc)

    deps = [conv_rows(r0) for r0 in range(0, tm, CONV_ROWS)]
    deps.append(_fold_to_anchor(szv[:, 0:ANCHOR_TILE[1]]))

    n_slabs = h_ref.shape[1] // MXU_K
    first_slab, last_slab = 6, 3 * n_slabs - 2
    anchors = [{}, {}, {}]
    for i, dep in enumerate(deps):
        slab = first_slab + (last_slab - first_slab) * i // (len(deps) - 1)
        anchors[slab // n_slabs][slab % n_slabs] = dep
    never = s < 0
    h = h_ref[...]
    ua_ref[...] = jnp.dot(h, _anchored_weights(wua_ref, anchors[0], never), preferred_element_type=F32)
    ug_ref[...] = jnp.dot(h, _anchored_weights(wug_ref, anchors[1], never), preferred_element_type=F32)
    zb_ref[...] = jnp.dot(h, _anchored_weights(wzb_ref, anchors[2], never), preferred_element_type=F32)

    @pl.when(jnp.logical_and(s > 0, jp == n_chunks - 1))
    def _():
        _layernorm_gate(conv_ref, sz_ref, lng_ref, lnb_ref, yb_ref)


def _branch_b_prompt(h, w_in, conv_w, conv_b, ln_g, ln_b, w_b, col0, *, tm, cw, seq_len):
    m, d = h.shape
    n_chunks = w_b // cw
    n_tiles = m // tm
    n_steps = n_tiles * n_chunks
    tiles_per_seq = seq_len // tm
    cur = lambda s: jnp.minimum(s, n_steps - 1)
    prev = lambda s: jnp.maximum(s - 1, 0)
    col = lambda k: (lambda s: (0, col0 // cw + k * n_chunks + cur(s) % n_chunks))
    in_specs = [pl.BlockSpec((tm, d), lambda s: (cur(s) // n_chunks, 0))]
    in_specs += [pl.BlockSpec((d, cw), col(k)) for k in (0, 1, 2)]
    in_specs += [pl.BlockSpec((CONV_B, cw), lambda s: (0, prev(s) % n_chunks)),
                 pl.BlockSpec((1, cw), lambda s: (0, prev(s) % n_chunks)),
                 pl.BlockSpec((1, w_b), lambda s: (0, 0)),
                 pl.BlockSpec((1, w_b), lambda s: (0, 0))]
    kernel = functools.partial(_branch_b_prompt_kernel, tiles_per_seq=tiles_per_seq,
                               n_chunks=n_chunks, n_steps=n_steps)
    yb, tail = pl.pallas_call(
        kernel, grid=(n_steps + 1,), in_specs=in_specs,
        out_specs=[pl.BlockSpec((tm, w_b), lambda s: (prev(s) // n_chunks, 0)),
                   pl.BlockSpec((1, HIST_B, cw), lambda s: (prev(s) // n_chunks, 0, prev(s) % n_chunks))],
        out_shape=[jax.ShapeDtypeStruct((m, w_b), BF16),
                   jax.ShapeDtypeStruct((n_tiles, HIST_B, w_b), F32)],
        scratch_shapes=[pltpu.VMEM((tm, cw), F32),
                        pltpu.VMEM((tm, cw), F32),
                        pltpu.VMEM((tm, cw), F32),
                        pltpu.VMEM((SUBLANES, tm + HIST_B, cw), F32),
                        pltpu.VMEM((n_chunks, HIST_B, cw), F32),
                        pltpu.VMEM((tm, w_b), F32),
                        pltpu.VMEM((tm, w_b), F32)],
        compiler_params=_params(1),
    )(h, w_in, w_in, w_in, conv_w, conv_b.reshape(1, w_b), ln_g.reshape(1, w_b), ln_b.reshape(1, w_b))
    return yb, tail[tiles_per_seq - 1::tiles_per_seq]


def _branch_b_sample_kernel(h_ref, wua_ref, wug_ref, wzb_ref, cw_ref, cb_ref, lng_ref, lnb_ref, st_ref,
                            yb_ref, u_ref, conv_ref, sz_ref):
    j = pl.program_id(1)
    cw = cw_ref.shape[1]
    h = h_ref[...]
    u = _dot(h, wua_ref) * jax.nn.sigmoid(_dot(h, wug_ref))
    u_ref[...] = u
    lanes = pl.ds(pl.multiple_of(j * cw, cw), cw)
    sz_ref[:, lanes] = _silu(_dot(h, wzb_ref))
    acc = cb_ref[...] + cw_ref[CONV_B - 1:CONV_B, :] * u
    for k in range(CONV_B - 1):
        acc = acc + cw_ref[k:k + 1, :] * st_ref[k]
    conv_ref[:, lanes] = acc

    @pl.when(j == pl.num_programs(1) - 1)
    def _():
        _layernorm_gate(conv_ref, sz_ref, lng_ref, lnb_ref, yb_ref)


def _branch_b_sample(h, w_in, conv_w, conv_b, ln_g, ln_b, w_b, col0, *, tm, cw, state):
    m, d = h.shape
    n_chunks = w_b // cw
    col = lambda k: (lambda i, j: (0, col0 // cw + k * n_chunks + j))
    in_specs = [pl.BlockSpec((tm, d), lambda i, j: (i, 0))]
    in_specs += [pl.BlockSpec((d, cw), col(k)) for k in (0, 1, 2)]
    in_specs += [pl.BlockSpec((CONV_B, cw), lambda i, j: (0, j)),
                 pl.BlockSpec((1, cw), lambda i, j: (0, j)),
                 pl.BlockSpec((1, w_b), lambda i, j: (0, 0)),
                 pl.BlockSpec((1, w_b), lambda i, j: (0, 0)),
                 pl.BlockSpec((CONV_B - 1, tm, cw), lambda i, j: (0, i, j))]
    return pl.pallas_call(
        _branch_b_sample_kernel, grid=(m // tm, n_chunks), in_specs=in_specs,
        out_specs=[pl.BlockSpec((tm, w_b), lambda i, j: (i, 0)),
                   pl.BlockSpec((tm, cw), lambda i, j: (i, j))],
        out_shape=[jax.ShapeDtypeStruct((m, w_b), BF16), jax.ShapeDtypeStruct((m, w_b), F32)],
        scratch_shapes=[pltpu.VMEM((tm, w_b), F32), pltpu.VMEM((tm, w_b), F32)],
        compiler_params=_params(2),
    )(h, w_in, w_in, w_in, conv_w, conv_b.reshape(1, w_b), ln_g.reshape(1, w_b), ln_b.reshape(1, w_b),
      jnp.transpose(state, (1, 0, 2)))


def _merge_kernel(h_ref, ya_ref, yb_ref, hx_ref, yax_ref, ybx_ref, wga_ref, wgb_ref, woa_ref, wob_ref,
                  mix_ref, mixx_ref):
    m, last = pl.program_id(0), pl.num_programs(0) - 1
    tm = h_ref.shape[0]

    def mix_of(h, ya, yb):
        return (jax.nn.sigmoid(_dot(h, wga_ref)) * _dot(ya, woa_ref)
                + jax.nn.sigmoid(_dot(h, wgb_ref)) * _dot(yb, wob_ref))

    @pl.when(m != last)
    def _():
        mix_ref[...] = mix_of(h_ref[...], ya_ref[...], yb_ref[...]).astype(mix_ref.dtype)

    @pl.when(m == last)
    def _():
        rows = lambda a, b: jnp.concatenate([a[...], b[...]], axis=0)
        mix = mix_of(rows(h_ref, hx_ref), rows(ya_ref, yax_ref), rows(yb_ref, ybx_ref))
        mix_ref[...] = mix[:tm].astype(mix_ref.dtype)
        mixx_ref[...] = mix[tm:].astype(mixx_ref.dtype)


def _merge(h, ya, yb, extra, w_in, w_out_a, w_out_b, col0, *, tm, cn):
    m, d = h.shape
    mx = extra[0].shape[0]
    w = ya.shape[1]
    n_chunks = d // cn
    n_tiles = m // tm
    gcol = lambda k: (lambda i, j: (0, col0 // cn + k * n_chunks + j))
    whole = lambda i, j: (0, 0)
    return pl.pallas_call(
        _merge_kernel,
        grid=(n_tiles, n_chunks),
        in_specs=[pl.BlockSpec((tm, d), lambda i, j: (i, 0)),
                  pl.BlockSpec((tm, w), lambda i, j: (i, 0)),
                  pl.BlockSpec((tm, w), lambda i, j: (i, 0)),
                  pl.BlockSpec((mx, d), whole),
                  pl.BlockSpec((mx, w), whole),
                  pl.BlockSpec((mx, w), whole),
                  pl.BlockSpec((d, cn), gcol(0)),
                  pl.BlockSpec((d, cn), gcol(1)),
                  pl.BlockSpec((w, cn), lambda i, j: (0, j)),
                  pl.BlockSpec((w, cn), lambda i, j: (0, j))],
        out_specs=[pl.BlockSpec((tm, cn), lambda i, j: (i, j)),
                   pl.BlockSpec((mx, cn), lambda i, j: (0, jnp.where(i == n_tiles - 1, j, 0)))],
        out_shape=[jax.ShapeDtypeStruct((m, d), BF16), jax.ShapeDtypeStruct((mx, d), BF16)],
        compiler_params=_params(2),
    )(h, ya, yb, *extra, w_in, w_in, w_out_a, w_out_b)


def _out_kernel(mix_ref, x_ref, wo_ref, g_ref, y_ref, ssq_ref):
    j = pl.program_id(1)
    cn = wo_ref.shape[1]
    o = x_ref[...] + _dot(mix_ref[...], wo_ref)
    y_ref[:, pl.ds(pl.multiple_of(j * cn, cn), cn)] = o
    part = jnp.sum(o * o, axis=-1, keepdims=True)

    @pl.when(j == 0)
    def _():
        ssq_ref[...] = part

    @pl.when(j != 0)
    def _():
        ssq_ref[...] += part

    @pl.when(j == pl.num_programs(1) - 1)
    def _():
        scale = lax.rsqrt(ssq_ref[...] / y_ref.shape[1] + EPS)
        y_ref[...] = y_ref[...] * scale * g_ref[...]


def _out_proj(mix, x, w_o, gain, *, tm, cn):
    m, d = x.shape
    return pl.pallas_call(
        _out_kernel,
        grid=(m // tm, d // cn),
        in_specs=[pl.BlockSpec((tm, d), lambda i, j: (i, 0)),
                  pl.BlockSpec((tm, cn), lambda i, j: (i, j)),
                  pl.BlockSpec((d, cn), lambda i, j: (0, j)),
                  pl.BlockSpec((1, d), lambda i, j: (0, 0))],
        out_specs=pl.BlockSpec((tm, d), lambda i, j: (i, 0)),
        out_shape=jax.ShapeDtypeStruct((m, d), F32),
        scratch_shapes=[pltpu.VMEM((tm, 1), F32)],
        compiler_params=_params(2),
    )(mix, x, w_o, gain.reshape(1, d))


def _tiles(m):
    if m >= 1024:
        return dict(norm=512, a=1024, b=512, merge=512, out=512, a_cw=256, b_cw=256, merge_cn=512)
    return dict(norm=m, a=m, b=m, merge=m, out=m, a_cw=256, b_cw=256, merge_cn=512)


def _branches(x, w, bf16_w=None, *, seq_len=None, state_a=None, state_b=None):
    m, d = x.shape
    w_a = w["conv_a_w"].shape[1]
    w_b = w["conv_b_w"].shape[1]
    t = _tiles(m)
    h = _rmsnorm_cast(x, w["norm_gain"], t["norm"])
    casts = ()
    if bf16_w is None:
        w_in = w["w_in"]
        casts = ((w_in, 4 * w_a, w_in.shape[1] - 4 * w_a), (w["w_out_a"], 0, d), (w["w_out_b"], 0, d),
                 (w["w_o"], 0, d))
    ya, a_tail, cast_out = _branch_a(h, w["w_in_a"], w["conv_a_w"], w["conv_a_b"], w_a,
                                     tm=t["a"], cw=t["a_cw"], seq_len=seq_len, state=state_a, casts=casts)
    if bf16_w is None:
        bf16_w = tuple(cast_out)
    w_rest, w_out_a, w_out_b, w_o = bf16_w
    b_args = (h, w_rest, w["conv_b_w"], w["conv_b_b"], w["ln_b_gain"], w["ln_b_bias"], w_b, 0)
    if state_b is None:
        yb, b_tail = _branch_b_prompt(*b_args, tm=t["b"], cw=t["b_cw"], seq_len=seq_len)
    else:
        yb, b_tail = _branch_b_sample(*b_args, tm=t["b"], cw=t["b_cw"], state=state_b)
    return (h, ya, yb), a_tail, b_tail, bf16_w


def kernel(x_prompt, x_sample, state_conv_a, state_conv_b, norm_gain, w_in, conv_a_w, conv_a_b, w_out_a,
           conv_b_w, conv_b_b, ln_b_gain, ln_b_bias, w_out_b, w_o, final_norm_gain):
    depth = norm_gain.shape[0]
    assert depth == 1, "the output projection kernel applies the final norm: one layer only"
    batch, seq, d = x_prompt.shape
    dec_batch, dec_seq, _ = x_sample.shape
    assert dec_seq == 1
    w_a = conv_a_w.shape[2]
    w = dict(norm_gain=norm_gain[0], w_in=w_in[0], w_in_a=w_in[0, :, :4 * w_a].astype(BF16),
             conv_a_w=conv_a_w[0], conv_a_b=conv_a_b[0], w_out_a=w_out_a[0], conv_b_w=conv_b_w[0],
             conv_b_b=conv_b_b[0], ln_b_gain=ln_b_gain[0], ln_b_bias=ln_b_bias[0], w_out_b=w_out_b[0],
             w_o=w_o[0], final_norm_gain=final_norm_gain)

    xp, xs = x_prompt.reshape(batch * seq, d), x_sample.reshape(dec_batch, d)
    acts_p, a_tail, b_tail, bf16_w = _branches(xp, w, seq_len=seq)
    new_a_prompt = a_tail[:, HIST_A - (CONV_A - 1):, :]
    new_b_prompt = b_tail[:, HIST_B - (CONV_B - 1):, :]

    sa, sb = state_conv_a[0], state_conv_b[0]
    acts_s, cv_s, u_s, _ = _branches(xs, w, bf16_w, state_a=sa, state_b=sb)
    new_a_sample = jnp.concatenate([sa[:, 1:, :], cv_s[:, None, :]], axis=1)
    new_b_sample = jnp.concatenate([sb[:, 1:, :], u_s[:, None, :]], axis=1)

    w_rest, w_out_a_bf, w_out_b_bf, w_o_bf = bf16_w
    tp, ts = _tiles(xp.shape[0]), _tiles(xs.shape[0])
    mix_p, mix_s = _merge(*acts_p, acts_s, w_rest, w_out_a_bf, w_out_b_bf, 3 * conv_b_w.shape[2],
                          tm=tp["merge"], cn=tp["merge_cn"])
    yp = _out_proj(mix_p, xp, w_o_bf, final_norm_gain, tm=tp["out"], cn=1024)
    ys = _out_proj(mix_s, xs, w_o_bf, final_norm_gain, tm=ts["out"], cn=1024)

    return (yp.reshape(batch, seq, d), ys.reshape(dec_batch, dec_seq, d),
            new_a_prompt[None], new_b_prompt[None], new_a_sample[None], new_b_sample[None])
```

```python
import functools

import jax
import jax.numpy as jnp
from jax import lax
from jax.experimental import pallas as pl
from jax.experimental.pallas import tpu as pltpu

EPS = 1e-6
CONV_A = 3
CONV_B = 31
SUBLANES = 8
HIST_A = SUBLANES
HIST_B = 4 * SUBLANES
VMEM_LIMIT = 60 * 1024 * 1024

F32 = jnp.float32
BF16 = jnp.bfloat16


def _params(n_axes):
    return pltpu.CompilerParams(
        dimension_semantics=("arbitrary",) * n_axes, vmem_limit_bytes=VMEM_LIMIT)


def _dot(a, w_ref):
    return jnp.dot(a, w_ref[...], preferred_element_type=F32)


def _silu(x):
    return x * jax.nn.sigmoid(x)


def _rmsnorm_kernel(x_ref, g_ref, h_ref):
    x = x_ref[...]
    ms = jnp.mean(x * x, axis=-1, keepdims=True)
    h_ref[...] = (x * lax.rsqrt(ms + EPS) * g_ref[...]).astype(h_ref.dtype)


def _rmsnorm_cast(x, gain, tm):
    m, d = x.shape
    return pl.pallas_call(
        _rmsnorm_kernel,
        grid=(m // tm,),
        in_specs=[pl.BlockSpec((tm, d), lambda i: (i, 0)),
                  pl.BlockSpec((1, d), lambda i: (0, 0))],
        out_specs=pl.BlockSpec((tm, d), lambda i: (i, 0)),
        out_shape=jax.ShapeDtypeStruct((m, d), BF16),
        compiler_params=_params(1),
    )(x, gain.reshape(1, d))


def _cast_slab(t, n_steps, jobs, srcs, dsts, inbufs, outbufs, in_sems, out_sems):
    def read(i, step, slot):
        rows, col0, cols = jobs[i]
        src = srcs[i].at[pl.ds(pl.multiple_of(step * rows, rows), rows), pl.ds(col0, cols)]
        return pltpu.make_async_copy(src, inbufs[i].at[slot], in_sems.at[slot, i])

    def write(i, step):
        rows = jobs[i][0]
        dst = dsts[i].at[pl.ds(pl.multiple_of(step * rows, rows), rows), :]
        return pltpu.make_async_copy(outbufs[i], dst, out_sems.at[i])

    n = len(jobs)
    slot = t % 2

    @pl.when(t == 0)
    def _():
        for i in range(n):
            read(i, 0, 0).start()

    for i in range(n):
        read(i, t, slot).wait()

    @pl.when(t + 1 < n_steps)
    def _():
        for i in range(n):
            read(i, t + 1, 1 - slot).start()

    @pl.when(t > 0)
    def _():
        for i in range(n):
            write(i, t - 1).wait()

    for i in range(n):
        outbufs[i][...] = inbufs[i][slot].astype(outbufs[i].dtype)
    for i in range(n):
        write(i, t).start()

    @pl.when(t == n_steps - 1)
    def _():
        for i in range(n):
            write(i, t).wait()


def _branch_a_prompt_kernel(h_ref, wb_ref, wc_ref, wv_ref, wz_ref, cw_ref, cb_ref, *rest,
                            tiles_per_seq, jobs):
    n = len(jobs)
    srcs, rest = rest[:n], rest[n:]
    ya_ref, cvlast_ref = rest[:2]
    dsts, rest = rest[2:2 + n], rest[2 + n:]
    buf_ref, carry_ref = rest[:2]
    inbufs, outbufs = rest[2:2 + n], rest[2 + n:2 + 2 * n]
    in_sems, out_sems = rest[2 + 2 * n:]
    m, j = pl.program_id(0), pl.program_id(1)
    n_steps = pl.num_programs(0) * pl.num_programs(1)
    _cast_slab(m * pl.num_programs(1) + j, n_steps, jobs, srcs, dsts, inbufs, outbufs, in_sems, out_sems)
    tm = h_ref.shape[0]
    h = h_ref[...]
    cv = _dot(h, wc_ref) * _dot(h, wv_ref)

    @pl.when(m % tiles_per_seq == 0)
    def _():
        buf_ref[0:HIST_A, :] = jnp.zeros((HIST_A, cv.shape[1]), F32)

    @pl.when(m % tiles_per_seq != 0)
    def _():
        buf_ref[0:HIST_A, :] = carry_ref[j]

    buf_ref[HIST_A:, :] = cv
    last = cv[tm - HIST_A:, :]
    carry_ref[j] = last
    cvlast_ref[0] = last
    never = m < 0
    rows = tm // A_BLOCKS
    convs, anchors = [], ({}, {})
    for i in range(A_BLOCKS):
        r0 = i * rows
        conv = (cw_ref[0:1, :] * buf_ref[r0 + HIST_A - 2:r0 + HIST_A - 2 + rows, :]
                + cw_ref[1:2, :] * buf_ref[r0 + HIST_A - 1:r0 + HIST_A - 1 + rows, :]
                + cw_ref[2:3, :] * cv[r0:r0 + rows, :] + cb_ref[...])
        convs.append(conv)
        anchors[i % 2][2 + 3 * (i // 2)] = _fold_to_anchor(conv)
    z = jnp.dot(h, _anchored_weights(wz_ref, anchors[0], never), preferred_element_type=F32)
    b = jnp.dot(h, _anchored_weights(wb_ref, anchors[1], never), preferred_element_type=F32)
    for i in range(A_BLOCKS):
        r0 = i * rows
        ya = b[r0:r0 + rows, :] * convs[i] * _silu(z[r0:r0 + rows, :])
        ya_ref[r0:r0 + rows, :] = ya.astype(ya_ref.dtype)


A_BLOCKS = 8


def _branch_a_sample_kernel(h_ref, wb_ref, wc_ref, wv_ref, wz_ref, cw_ref, cb_ref, s0_ref, s1_ref,
                            ya_ref, cv_ref):
    h = h_ref[...]
    cv = _dot(h, wc_ref) * _dot(h, wv_ref)
    cv_ref[...] = cv
    conv = (cw_ref[0:1, :] * s0_ref[...] + cw_ref[1:2, :] * s1_ref[...]
            + cw_ref[2:3, :] * cv + cb_ref[...])
    ya = _dot(h, wb_ref) * conv * _silu(_dot(h, wz_ref))
    ya_ref[...] = ya.astype(ya_ref.dtype)


def _branch_a(h, w_in, conv_w, conv_b, w_a, *, tm, cw, seq_len=None, state=None, casts=()):
    m, d = h.shape
    n_chunks = w_a // cw
    col = lambda k: (lambda i, j: (0, k * n_chunks + j))
    in_specs = [pl.BlockSpec((tm, d), lambda i, j: (i, 0))]
    in_specs += [pl.BlockSpec((d, cw), col(k)) for k in (0, 1, 2, 3)]
    in_specs += [pl.BlockSpec((CONV_A, cw), lambda i, j: (0, j)),
                 pl.BlockSpec((1, cw), lambda i, j: (0, j))]
    args = [h, w_in, w_in, w_in, w_in, conv_w, conv_b.reshape(1, w_a)]
    ya_spec = pl.BlockSpec((tm, cw), lambda i, j: (i, j))
    ya_shape = jax.ShapeDtypeStruct((m, w_a), BF16)
    if state is None:
        tiles_per_seq = seq_len // tm
        n_steps = (m // tm) * n_chunks
        jobs = tuple((src.shape[0] // n_steps, col0, cols) for src, col0, cols in casts)
        assert all(src.shape[0] == rows * n_steps and rows % 16 == 0
                   for (src, _, _), (rows, _, _) in zip(casts, jobs))
        any_spec = pl.BlockSpec(memory_space=pl.ANY)
        kernel = functools.partial(_branch_a_prompt_kernel, tiles_per_seq=tiles_per_seq, jobs=jobs)
        in_specs += [any_spec] * len(jobs)
        args += [src for src, _, _ in casts]
        out_specs = [ya_spec, pl.BlockSpec((1, HIST_A, cw), lambda i, j: (i, 0, j))] + [any_spec] * len(jobs)
        out_shape = [ya_shape, jax.ShapeDtypeStruct((m // tm, HIST_A, w_a), F32)]
        out_shape += [jax.ShapeDtypeStruct((src.shape[0], cols), BF16) for src, _, cols in casts]
        scratch = [pltpu.VMEM((tm + HIST_A, cw), F32), pltpu.VMEM((n_chunks, HIST_A, cw), F32)]
        scratch += [pltpu.VMEM((2, rows, cols), F32) for rows, _, cols in jobs]
        scratch += [pltpu.VMEM((rows, cols), BF16) for rows, _, cols in jobs]
        scratch += [pltpu.SemaphoreType.DMA((2, len(jobs))), pltpu.SemaphoreType.DMA((len(jobs),))]
    else:
        kernel = _branch_a_sample_kernel
        in_specs += [pl.BlockSpec((tm, cw), lambda i, j: (i, j))] * 2
        args += [state[:, 0, :], state[:, 1, :]]
        out_specs = [ya_spec, pl.BlockSpec((tm, cw), lambda i, j: (i, j))]
        out_shape = [ya_shape, jax.ShapeDtypeStruct((m, w_a), F32)]
        scratch = []
    ya, tail, *cast_out = pl.pallas_call(
        kernel, grid=(m // tm, n_chunks), in_specs=in_specs, out_specs=out_specs,
        out_shape=out_shape, scratch_shapes=scratch, compiler_params=_params(2),
    )(*args)
    if state is None:
        tail = tail[tiles_per_seq - 1::tiles_per_seq]
    return ya, tail, cast_out


LN_ROWS = 16
LN_UNROLL = 8


def _layernorm_gate(conv_ref, sz_ref, lng_ref, lnb_ref, yb_ref):
    tm, w = conv_ref.shape
    g = lng_ref[...]
    b = lnb_ref[...]

    def body(i, _):
        r0 = pl.multiple_of(i * LN_ROWS, LN_ROWS)
        x = conv_ref[pl.ds(r0, LN_ROWS), :]
        mu = jnp.mean(x, axis=-1, keepdims=True)
        xc = x - mu
        var = jnp.mean(xc * xc, axis=-1, keepdims=True)
        r = xc * lax.rsqrt(var + EPS) * g + b
        yb_ref[pl.ds(r0, LN_ROWS), :] = (_silu(r) * sz_ref[pl.ds(r0, LN_ROWS), :]).astype(yb_ref.dtype)
        return 0

    lax.fori_loop(0, tm // LN_ROWS, body, 0, unroll=LN_UNROLL)


CONV_ROWS = 32


MXU_K = 256
ANCHOR_TILE = (16, 128)


def _anchored_weights(w_ref, anchors, never):
    ar, ac = ANCHOR_TILE
    pieces = []
    for i in range(w_ref.shape[0] // MXU_K):
        r0 = i * MXU_K
        if i not in anchors:
            pieces.append(w_ref[r0:r0 + MXU_K, :])
            continue
        first = jnp.where(never, anchors[i].astype(w_ref.dtype), w_ref[r0:r0 + ar, 0:ac])
        top = jnp.concatenate([first, w_ref[r0:r0 + ar, ac:]], axis=1)
        pieces.append(jnp.concatenate([top, w_ref[r0 + ar:r0 + MXU_K, :]], axis=0))
    return jnp.concatenate(pieces, axis=0)


def _fold_to_anchor(x):
    ar, ac = ANCHOR_TILE
    tiles = [x[r:r + ar, c:c + ac] for r in range(0, x.shape[0], ar) for c in range(0, x.shape[1], ac)]
    out = tiles[0]
    for t in tiles[1:]:
        out = out + t
    return out


def _branch_b_prompt_kernel(h_ref, wua_ref, wug_ref, wzb_ref, cw_ref, cb_ref, lng_ref, lnb_ref,
                            yb_ref, ulast_ref, ua_ref, ug_ref, zb_ref, ush_ref, carry_ref, conv_ref, sz_ref,
                            *, tiles_per_seq, n_chunks, n_steps):
    s = pl.program_id(0)
    prev = jnp.maximum(s - 1, 0)
    mp, jp = prev // n_chunks, prev % n_chunks
    tm = h_ref.shape[0]
    cw = cw_ref.shape[1]
    n_rows = tm + HIST_B
    lanes_p = pl.ds(pl.multiple_of(jp * cw, cw), cw)

    @pl.when(s == 0)
    def _():
        for ref in (ua_ref, ug_ref, zb_ref, carry_ref):
            ref[...] = jnp.zeros(ref.shape, F32)

    u = ua_ref[...] * jax.nn.sigmoid(ug_ref[...])
    szv = _silu(zb_ref[...])
    sz_ref[:, lanes_p] = szv
    hist = jnp.where(mp % tiles_per_seq != 0, carry_ref[jp], 0.0)
    tail = u[tm - HIST_B:, :]
    carry_ref[jp] = tail
    ulast_ref[0] = tail
    staged = jnp.concatenate([hist, u], axis=0)
    ush_ref[0] = staged
    for r in range(1, SUBLANES):
        ush_ref[r] = pltpu.roll(staged, n_rows - r, axis=0)

    first = HIST_B - (CONV_B - 1)

    def conv_rows(r0):
        acc = cb_ref[...]
        for k in range(CONV_B):
            r, q = (first + k) % SUBLANES, (first + k) // SUBLANES
            acc = acc + cw_ref[k:k + 1, :] * ush_ref[r, r0 + q * SUBLANES:r0 + q * SUBLANES + CONV_ROWS, :]
        conv_ref[r0:r0 + CONV_ROWS, lanes_p] = acc
        return _fold_to_anchor(acc)

    deps = [conv_rows(r0) for r0 in range(0, tm, CONV_ROWS)]
    deps.append(_fold_to_anchor(szv[:, 0:ANCHOR_TILE[1]]))

    n_slabs = h_ref.shape[1] // MXU_K
    first_slab, last_slab = 7, 3 * n_slabs - 1
    anchors = [{}, {}, {}]
    for i, dep in enumerate(deps):
        slab = first_slab + (last_slab - first_slab) * i // (len(deps) - 1)
        anchors[slab // n_slabs][slab % n_slabs] = dep
    never = s < 0
    h = h_ref[...]
    ua_ref[...] = jnp.dot(h, _anchored_weights(wua_ref, anchors[0], never), preferred_element_type=F32)
    ug_ref[...] = jnp.dot(h, _anchored_weights(wug_ref, anchors[1], never), preferred_element_type=F32)
    zb_ref[...] = jnp.dot(h, _anchored_weights(wzb_ref, anchors[2], never), preferred_element_type=F32)

    @pl.when(jnp.logical_and(s > 0, jp == n_chunks - 1))
    def _():
        _layernorm_gate(conv_ref, sz_ref, lng_ref, lnb_ref, yb_ref)


def _branch_b_prompt(h, w_in, conv_w, conv_b, ln_g, ln_b, w_b, col0, *, tm, cw, seq_len):
    m, d = h.shape
    n_chunks = w_b // cw
    n_tiles = m // tm
    n_steps = n_tiles * n_chunks
    tiles_per_seq = seq_len // tm
    cur = lambda s: jnp.minimum(s, n_steps - 1)
    prev = lambda s: jnp.maximum(s - 1, 0)
    col = lambda k: (lambda s: (0, col0 // cw + k * n_chunks + cur(s) % n_chunks))
    in_specs = [pl.BlockSpec((tm, d), lambda s: (cur(s) // n_chunks, 0))]
    in_specs += [pl.BlockSpec((d, cw), col(k)) for k in (0, 1, 2)]
    in_specs += [pl.BlockSpec((CONV_B, cw), lambda s: (0, prev(s) % n_chunks)),
                 pl.BlockSpec((1, cw), lambda s: (0, prev(s) % n_chunks)),
                 pl.BlockSpec((1, w_b), lambda s: (0, 0)),
                 pl.BlockSpec((1, w_b), lambda s: (0, 0))]
    kernel = functools.partial(_branch_b_prompt_kernel, tiles_per_seq=tiles_per_seq,
                               n_chunks=n_chunks, n_steps=n_steps)
    yb, tail = pl.pallas_call(
        kernel, grid=(n_steps + 1,), in_specs=in_specs,
        out_specs=[pl.BlockSpec((tm, w_b), lambda s: (prev(s) // n_chunks, 0)),
                   pl.BlockSpec((1, HIST_B, cw), lambda s: (prev(s) // n_chunks, 0, prev(s) % n_chunks))],
        out_shape=[jax.ShapeDtypeStruct((m, w_b), BF16),
                   jax.ShapeDtypeStruct((n_tiles, HIST_B, w_b), F32)],
        scratch_shapes=[pltpu.VMEM((tm, cw), F32),
                        pltpu.VMEM((tm, cw), F32),
                        pltpu.VMEM((tm, cw), F32),
                        pltpu.VMEM((SUBLANES, tm + HIST_B, cw), F32),
                        pltpu.VMEM((n_chunks, HIST_B, cw), F32),
                        pltpu.VMEM((tm, w_b), F32),
                        pltpu.VMEM((tm, w_b), F32)],
        compiler_params=_params(1),
    )(h, w_in, w_in, w_in, conv_w, conv_b.reshape(1, w_b), ln_g.reshape(1, w_b), ln_b.reshape(1, w_b))
    return yb, tail[tiles_per_seq - 1::tiles_per_seq]


def _branch_b_sample_kernel(h_ref, wua_ref, wug_ref, wzb_ref, cw_ref, cb_ref, lng_ref, lnb_ref, st_ref,
                            yb_ref, u_ref, conv_ref, sz_ref):
    j = pl.program_id(1)
    cw = cw_ref.shape[1]
    h = h_ref[...]
    u = _dot(h, wua_ref) * jax.nn.sigmoid(_dot(h, wug_ref))
    u_ref[...] = u
    lanes = pl.ds(pl.multiple_of(j * cw, cw), cw)
    sz_ref[:, lanes] = _silu(_dot(h, wzb_ref))
    acc = cb_ref[...] + cw_ref[CONV_B - 1:CONV_B, :] * u
    for k in range(CONV_B - 1):
        acc = acc + cw_ref[k:k + 1, :] * st_ref[k]
    conv_ref[:, lanes] = acc

    @pl.when(j == pl.num_programs(1) - 1)
    def _():
        _layernorm_gate(conv_ref, sz_ref, lng_ref, lnb_ref, yb_ref)


def _branch_b_sample(h, w_in, conv_w, conv_b, ln_g, ln_b, w_b, col0, *, tm, cw, state):
    m, d = h.shape
    n_chunks = w_b // cw
    col = lambda k: (lambda i, j: (0, col0 // cw + k * n_chunks + j))
    in_specs = [pl.BlockSpec((tm, d), lambda i, j: (i, 0))]
    in_specs += [pl.BlockSpec((d, cw), col(k)) for k in (0, 1, 2)]
    in_specs += [pl.BlockSpec((CONV_B, cw), lambda i, j: (0, j)),
                 pl.BlockSpec((1, cw), lambda i, j: (0, j)),
                 pl.BlockSpec((1, w_b), lambda i, j: (0, 0)),
                 pl.BlockSpec((1, w_b), lambda i, j: (0, 0)),
                 pl.BlockSpec((CONV_B - 1, tm, cw), lambda i, j: (0, i, j))]
    return pl.pallas_call(
        _branch_b_sample_kernel, grid=(m // tm, n_chunks), in_specs=in_specs,
        out_specs=[pl.BlockSpec((tm, w_b), lambda i, j: (i, 0)),
                   pl.BlockSpec((tm, cw), lambda i, j: (i, j))],
        out_shape=[jax.ShapeDtypeStruct((m, w_b), BF16), jax.ShapeDtypeStruct((m, w_b), F32)],
        scratch_shapes=[pltpu.VMEM((tm, w_b), F32), pltpu.VMEM((tm, w_b), F32)],
        compiler_params=_params(2),
    )(h, w_in, w_in, w_in, conv_w, conv_b.reshape(1, w_b), ln_g.reshape(1, w_b), ln_b.reshape(1, w_b),
      jnp.transpose(state, (1, 0, 2)))


def _merge_kernel(h_ref, ya_ref, yb_ref, hx_ref, yax_ref, ybx_ref, wga_ref, wgb_ref, woa_ref, wob_ref,
                  mix_ref, mixx_ref):
    m, last = pl.program_id(0), pl.num_programs(0) - 1
    tm = h_ref.shape[0]

    def mix_of(h, ya, yb):
        return (jax.nn.sigmoid(_dot(h, wga_ref)) * _dot(ya, woa_ref)
                + jax.nn.sigmoid(_dot(h, wgb_ref)) * _dot(yb, wob_ref))

    @pl.when(m != last)
    def _():
        mix_ref[...] = mix_of(h_ref[...], ya_ref[...], yb_ref[...]).astype(mix_ref.dtype)

    @pl.when(m == last)
    def _():
        rows = lambda a, b: jnp.concatenate([a[...], b[...]], axis=0)
        mix = mix_of(rows(h_ref, hx_ref), rows(ya_ref, yax_ref), rows(yb_ref, ybx_ref))
        mix_ref[...] = mix[:tm].astype(mix_ref.dtype)
        mixx_ref[...] = mix[tm:].astype(mixx_ref.dtype)


def _merge(h, ya, yb, extra, w_in, w_out_a, w_out_b, col0, *, tm, cn):
    m, d = h.shape
    mx = extra[0].shape[0]
    w = ya.shape[1]
    n_chunks = d // cn
    n_tiles = m // tm
    gcol = lambda k: (lambda i, j: (0, col0 // cn + k * n_chunks + j))
    whole = lambda i, j: (0, 0)
    return pl.pallas_call(
        _merge_kernel,
        grid=(n_tiles, n_chunks),
        in_specs=[pl.BlockSpec((tm, d), lambda i, j: (i, 0)),
                  pl.BlockSpec((tm, w), lambda i, j: (i, 0)),
                  pl.BlockSpec((tm, w), lambda i, j: (i, 0)),
                  pl.BlockSpec((mx, d), whole),
                  pl.BlockSpec((mx, w), whole),
                  pl.BlockSpec((mx, w), whole),
                  pl.BlockSpec((d, cn), gcol(0)),
                  pl.BlockSpec((d, cn), gcol(1)),
                  pl.BlockSpec((w, cn), lambda i, j: (0, j)),
                  pl.BlockSpec((w, cn), lambda i, j: (0, j))],
        out_specs=[pl.BlockSpec((tm, cn), lambda i, j: (i, j)),
                   pl.BlockSpec((mx, cn), lambda i, j: (0, jnp.where(i == n_tiles - 1, j, 0)))],
        out_shape=[jax.ShapeDtypeStruct((m, d), BF16), jax.ShapeDtypeStruct((mx, d), BF16)],
        compiler_params=_params(2),
    )(h, ya, yb, *extra, w_in, w_in, w_out_a, w_out_b)


def _out_kernel(mix_ref, x_ref, wo_ref, g_ref, y_ref, ssq_ref):
    j = pl.program_id(1)
    cn = wo_ref.shape[1]
    o = x_ref[...] + _dot(mix_ref[...], wo_ref)
    y_ref[:, pl.ds(pl.multiple_of(j * cn, cn), cn)] = o
    part = jnp.sum(o * o, axis=-1, keepdims=True)

    @pl.when(j == 0)
    def _():
        ssq_ref[...] = part

    @pl.when(j != 0)
    def _():
        ssq_ref[...] += part

    @pl.when(j == pl.num_programs(1) - 1)
    def _():
        scale = lax.rsqrt(ssq_ref[...] / y_ref.shape[1] + EPS)
        y_ref[...] = y_ref[...] * scale * g_ref[...]


def _out_proj(mix, x, w_o, gain, *, tm, cn):
    m, d = x.shape
    return pl.pallas_call(
        _out_kernel,
        grid=(m // tm, d // cn),
        in_specs=[pl.BlockSpec((tm, d), lambda i, j: (i, 0)),
                  pl.BlockSpec((tm, cn), lambda i, j: (i, j)),
                  pl.BlockSpec((d, cn), lambda i, j: (0, j)),
                  pl.BlockSpec((1, d), lambda i, j: (0, 0))],
        out_specs=pl.BlockSpec((tm, d), lambda i, j: (i, 0)),
        out_shape=jax.ShapeDtypeStruct((m, d), F32),
        scratch_shapes=[pltpu.VMEM((tm, 1), F32)],
        compiler_params=_params(2),
    )(mix, x, w_o, gain.reshape(1, d))


def _tiles(m):
    if m >= 1024:
        return dict(norm=512, a=1024, b=512, merge=512, out=512, a_cw=256, b_cw=256, merge_cn=512)
    return dict(norm=m, a=m, b=m, merge=m, out=m, a_cw=256, b_cw=256, merge_cn=512)


def _branches(x, w, bf16_w=None, *, seq_len=None, state_a=None, state_b=None):
    m, d = x.shape
    w_a = w["conv_a_w"].shape[1]
    w_b = w["conv_b_w"].shape[1]
    t = _tiles(m)
    h = _rmsnorm_cast(x, w["norm_gain"], t["norm"])
    casts = ()
    if bf16_w is None:
        w_in = w["w_in"]
        casts = ((w_in, 4 * w_a, w_in.shape[1] - 4 * w_a), (w["w_out_a"], 0, d), (w["w_out_b"], 0, d),
                 (w["w_o"], 0, d))
    ya, a_tail, cast_out = _branch_a(h, w["w_in_a"], w["conv_a_w"], w["conv_a_b"], w_a,
                                     tm=t["a"], cw=t["a_cw"], seq_len=seq_len, state=state_a, casts=casts)
    if bf16_w is None:
        bf16_w = tuple(cast_out)
    w_rest, w_out_a, w_out_b, w_o = bf16_w
    b_args = (h, w_rest, w["conv_b_w"], w["conv_b_b"], w["ln_b_gain"], w["ln_b_bias"], w_b, 0)
    if state_b is None:
        yb, b_tail = _branch_b_prompt(*b_args, tm=t["b"], cw=t["b_cw"], seq_len=seq_len)
    else:
        yb, b_tail = _branch_b_sample(*b_args, tm=t["b"], cw=t["b_cw"], state=state_b)
    return (h, ya, yb), a_tail, b_tail, bf16_w


def kernel(x_prompt, x_sample, state_conv_a, state_conv_b, norm_gain, w_in, conv_a_w, conv_a_b, w_out_a,
           conv_b_w, conv_b_b, ln_b_gain, ln_b_bias, w_out_b, w_o, final_norm_gain):
    depth = norm_gain.shape[0]
    assert depth == 1, "the output projection kernel applies the final norm: one layer only"
    batch, seq, d = x_prompt.shape
    dec_batch, dec_seq, _ = x_sample.shape
    assert dec_seq == 1
    w_a = conv_a_w.shape[2]
    w = dict(norm_gain=norm_gain[0], w_in=w_in[0], w_in_a=w_in[0, :, :4 * w_a].astype(BF16),
             conv_a_w=conv_a_w[0], conv_a_b=conv_a_b[0], w_out_a=w_out_a[0], conv_b_w=conv_b_w[0],
             conv_b_b=conv_b_b[0], ln_b_gain=ln_b_gain[0], ln_b_bias=ln_b_bias[0], w_out_b=w_out_b[0],
             w_o=w_o[0], final_norm_gain=final_norm_gain)

    xp, xs = x_prompt.reshape(batch * seq, d), x_sample.reshape(dec_batch, d)
    acts_p, a_tail, b_tail, bf16_w = _branches(xp, w, seq_len=seq)
    new_a_prompt = a_tail[:, HIST_A - (CONV_A - 1):, :]
    new_b_prompt = b_tail[:, HIST_B - (CONV_B - 1):, :]

    sa, sb = state_conv_a[0], state_conv_b[0]
    acts_s, cv_s, u_s, _ = _branches(xs, w, bf16_w, state_a=sa, state_b=sb)
    new_a_sample = jnp.concatenate([sa[:, 1:, :], cv_s[:, None, :]], axis=1)
    new_b_sample = lax.pad(sb, jnp.zeros((), sb.dtype), [(0, 0, 0), (-1, 1, 0), (0, 0, 0)])
    new_b_sample = new_b_sample.at[:, -1, :].set(u_s)

    w_rest, w_out_a_bf, w_out_b_bf, w_o_bf = bf16_w
    tp, ts = _tiles(xp.shape[0]), _tiles(xs.shape[0])
    mix_p, mix_s = _merge(*acts_p, acts_s, w_rest, w_out_a_bf, w_out_b_bf, 3 * conv_b_w.shape[2],
                          tm=tp["merge"], cn=tp["merge_cn"])
    yp = _out_proj(mix_p, xp, w_o_bf, final_norm_gain, tm=tp["out"], cn=1024)
    ys = _out_proj(mix_s, xs, w_o_bf, final_norm_gain, tm=ts["out"], cn=1024)

    return (yp.reshape(batch, seq, d), ys.reshape(dec_batch, dec_seq, d),
            new_a_prompt[None], new_b_prompt[None], new_a_sample[None], new_b_sample[None])
```

```python
import functools

import jax
import jax.numpy as jnp
from jax import lax
from jax.experimental import pallas as pl
from jax.experimental.pallas import tpu as pltpu

EPS = 1e-6
CONV_A = 3
CONV_B = 31
SUBLANES = 8
HIST_A = SUBLANES
HIST_B = 4 * SUBLANES
VMEM_LIMIT = 60 * 1024 * 1024

F32 = jnp.float32
BF16 = jnp.bfloat16


def _params(n_axes):
    return pltpu.CompilerParams(
        dimension_semantics=("arbitrary",) * n_axes, vmem_limit_bytes=VMEM_LIMIT)


def _dot(a, w_ref):
    return jnp.dot(a, w_ref[...], preferred_element_type=F32)


def _silu(x):
    return x * jax.nn.sigmoid(x)


def _rmsnorm_kernel(x_ref, g_ref, h_ref):
    x = x_ref[...]
    ms = jnp.mean(x * x, axis=-1, keepdims=True)
    h_ref[...] = (x * lax.rsqrt(ms + EPS) * g_ref[...]).astype(h_ref.dtype)


def _rmsnorm_cast(x, gain, tm):
    m, d = x.shape
    return pl.pallas_call(
        _rmsnorm_kernel,
        grid=(m // tm,),
        in_specs=[pl.BlockSpec((tm, d), lambda i: (i, 0)),
                  pl.BlockSpec((1, d), lambda i: (0, 0))],
        out_specs=pl.BlockSpec((tm, d), lambda i: (i, 0)),
        out_shape=jax.ShapeDtypeStruct((m, d), BF16),
        compiler_params=_params(1),
    )(x, gain.reshape(1, d))


def _cast_slab(t, n_steps, jobs, srcs, dsts, inbufs, outbufs, in_sems, out_sems):
    def read(i, step, slot):
        rows, col0, cols = jobs[i]
        src = srcs[i].at[pl.ds(pl.multiple_of(step * rows, rows), rows), pl.ds(col0, cols)]
        return pltpu.make_async_copy(src, inbufs[i].at[slot], in_sems.at[slot, i])

    def write(i, step):
        rows = jobs[i][0]
        dst = dsts[i].at[pl.ds(pl.multiple_of(step * rows, rows), rows), :]
        return pltpu.make_async_copy(outbufs[i], dst, out_sems.at[i])

    n = len(jobs)
    slot = t % 2

    @pl.when(t == 0)
    def _():
        for i in range(n):
            read(i, 0, 0).start()

    for i in range(n):
        read(i, t, slot).wait()

    @pl.when(t + 1 < n_steps)
    def _():
        for i in range(n):
            read(i, t + 1, 1 - slot).start()

    @pl.when(t > 0)
    def _():
        for i in range(n):
            write(i, t - 1).wait()

    deps = []
    for i in range(n):
        vals = inbufs[i][slot].astype(outbufs[i].dtype)
        outbufs[i][...] = vals
        deps.append(_fold_to_anchor(vals))

    def finish():
        for i in range(n):
            write(i, t).start()

        @pl.when(t == n_steps - 1)
        def _():
            for i in range(n):
                write(i, t).wait()

    return deps, finish


def _branch_a_prompt_kernel(h_ref, wb_ref, wc_ref, wv_ref, wz_ref, cw_ref, cb_ref, *rest,
                            tiles_per_seq, jobs):
    n = len(jobs)
    srcs, rest = rest[:n], rest[n:]
    ya_ref, cvlast_ref = rest[:2]
    dsts, rest = rest[2:2 + n], rest[2 + n:]
    buf_ref, carry_ref = rest[:2]
    inbufs, outbufs = rest[2:2 + n], rest[2 + n:2 + 2 * n]
    in_sems, out_sems = rest[2 + 2 * n:]
    m, j = pl.program_id(0), pl.program_id(1)
    n_steps = pl.num_programs(0) * pl.num_programs(1)
    cast_deps, finish_cast = _cast_slab(m * pl.num_programs(1) + j, n_steps, jobs, srcs, dsts,
                                        inbufs, outbufs, in_sems, out_sems)
    tm = h_ref.shape[0]
    h = h_ref[...]
    never = m < 0
    anchors = ({}, {})
    for i, dep in enumerate(cast_deps):
        anchors[i % 2][6 + 3 * (i // 2)] = dep
    cv = (jnp.dot(h, _anchored_weights(wc_ref, anchors[0], never), preferred_element_type=F32)
          * jnp.dot(h, _anchored_weights(wv_ref, anchors[1], never), preferred_element_type=F32))
    finish_cast()

    @pl.when(m % tiles_per_seq == 0)
    def _():
        buf_ref[0:HIST_A, :] = jnp.zeros((HIST_A, cv.shape[1]), F32)

    @pl.when(m % tiles_per_seq != 0)
    def _():
        buf_ref[0:HIST_A, :] = carry_ref[j]

    buf_ref[HIST_A:, :] = cv
    last = cv[tm - HIST_A:, :]
    carry_ref[j] = last
    cvlast_ref[0] = last
    rows = tm // A_BLOCKS
    convs, anchors = [], ({}, {})
    for i in range(A_BLOCKS):
        r0 = i * rows
        conv = (cw_ref[0:1, :] * buf_ref[r0 + HIST_A - 2:r0 + HIST_A - 2 + rows, :]
                + cw_ref[1:2, :] * buf_ref[r0 + HIST_A - 1:r0 + HIST_A - 1 + rows, :]
                + cw_ref[2:3, :] * cv[r0:r0 + rows, :] + cb_ref[...])
        convs.append(conv)
        anchors[i % 2][2 + 3 * (i // 2)] = _fold_to_anchor(conv)
    z = jnp.dot(h, _anchored_weights(wz_ref, anchors[0], never), preferred_element_type=F32)
    b = jnp.dot(h, _anchored_weights(wb_ref, anchors[1], never), preferred_element_type=F32)
    for i in range(A_BLOCKS):
        r0 = i * rows
        ya = b[r0:r0 + rows, :] * convs[i] * _silu(z[r0:r0 + rows, :])
        ya_ref[r0:r0 + rows, :] = ya.astype(ya_ref.dtype)


A_BLOCKS = 8


def _branch_a_sample_kernel(h_ref, wb_ref, wc_ref, wv_ref, wz_ref, cw_ref, cb_ref, s0_ref, s1_ref,
                            ya_ref, cv_ref):
    h = h_ref[...]
    cv = _dot(h, wc_ref) * _dot(h, wv_ref)
    cv_ref[...] = cv
    conv = (cw_ref[0:1, :] * s0_ref[...] + cw_ref[1:2, :] * s1_ref[...]
            + cw_ref[2:3, :] * cv + cb_ref[...])
    ya = _dot(h, wb_ref) * conv * _silu(_dot(h, wz_ref))
    ya_ref[...] = ya.astype(ya_ref.dtype)


def _branch_a(h, w_in, conv_w, conv_b, w_a, *, tm, cw, seq_len=None, state=None, casts=()):
    m, d = h.shape
    n_chunks = w_a // cw
    col = lambda k: (lambda i, j: (0, k * n_chunks + j))
    in_specs = [pl.BlockSpec((tm, d), lambda i, j: (i, 0))]
    in_specs += [pl.BlockSpec((d, cw), col(k)) for k in (0, 1, 2, 3)]
    in_specs += [pl.BlockSpec((CONV_A, cw), lambda i, j: (0, j)),
                 pl.BlockSpec((1, cw), lambda i, j: (0, j))]
    args = [h, w_in, w_in, w_in, w_in, conv_w, conv_b.reshape(1, w_a)]
    ya_spec = pl.BlockSpec((tm, cw), lambda i, j: (i, j))
    ya_shape = jax.ShapeDtypeStruct((m, w_a), BF16)
    if state is None:
        tiles_per_seq = seq_len // tm
        n_steps = (m // tm) * n_chunks
        jobs = tuple((src.shape[0] // n_steps, col0, cols) for src, col0, cols in casts)
        assert all(src.shape[0] == rows * n_steps and rows % 16 == 0
                   for (src, _, _), (rows, _, _) in zip(casts, jobs))
        any_spec = pl.BlockSpec(memory_space=pl.ANY)
        kernel = functools.partial(_branch_a_prompt_kernel, tiles_per_seq=tiles_per_seq, jobs=jobs)
        in_specs += [any_spec] * len(jobs)
        args += [src for src, _, _ in casts]
        out_specs = [ya_spec, pl.BlockSpec((1, HIST_A, cw), lambda i, j: (i, 0, j))] + [any_spec] * len(jobs)
        out_shape = [ya_shape, jax.ShapeDtypeStruct((m // tm, HIST_A, w_a), F32)]
        out_shape += [jax.ShapeDtypeStruct((src.shape[0], cols), BF16) for src, _, cols in casts]
        scratch = [pltpu.VMEM((tm + HIST_A, cw), F32), pltpu.VMEM((n_chunks, HIST_A, cw), F32)]
        scratch += [pltpu.VMEM((2, rows, cols), F32) for rows, _, cols in jobs]
        scratch += [pltpu.VMEM((rows, cols), BF16) for rows, _, cols in jobs]
        scratch += [pltpu.SemaphoreType.DMA((2, len(jobs))), pltpu.SemaphoreType.DMA((len(jobs),))]
    else:
        kernel = _branch_a_sample_kernel
        in_specs += [pl.BlockSpec((tm, cw), lambda i, j: (i, j))] * 2
        args += [state[:, 0, :], state[:, 1, :]]
        out_specs = [ya_spec, pl.BlockSpec((tm, cw), lambda i, j: (i, j))]
        out_shape = [ya_shape, jax.ShapeDtypeStruct((m, w_a), F32)]
        scratch = []
    ya, tail, *cast_out = pl.pallas_call(
        kernel, grid=(m // tm, n_chunks), in_specs=in_specs, out_specs=out_specs,
        out_shape=out_shape, scratch_shapes=scratch, compiler_params=_params(2),
    )(*args)
    if state is None:
        tail = tail[tiles_per_seq - 1::tiles_per_seq]
    return ya, tail, cast_out


LN_ROWS = 16
LN_UNROLL = 8


def _layernorm_gate(conv_ref, sz_ref, lng_ref, lnb_ref, yb_ref):
    tm, w = conv_ref.shape
    g = lng_ref[...]
    b = lnb_ref[...]

    def body(i, _):
        r0 = pl.multiple_of(i * LN_ROWS, LN_ROWS)
        x = conv_ref[pl.ds(r0, LN_ROWS), :]
        mu = jnp.mean(x, axis=-1, keepdims=True)
        xc = x - mu
        var = jnp.mean(xc * xc, axis=-1, keepdims=True)
        r = xc * lax.rsqrt(var + EPS) * g + b
        yb_ref[pl.ds(r0, LN_ROWS), :] = (_silu(r) * sz_ref[pl.ds(r0, LN_ROWS), :]).astype(yb_ref.dtype)
        return 0

    lax.fori_loop(0, tm // LN_ROWS, body, 0, unroll=LN_UNROLL)


CONV_ROWS = 32


MXU_K = 256
ANCHOR_TILE = (16, 128)


def _anchored_weights(w_ref, anchors, never):
    ar, ac = ANCHOR_TILE
    pieces = []
    for i in range(w_ref.shape[0] // MXU_K):
        r0 = i * MXU_K
        if i not in anchors:
            pieces.append(w_ref[r0:r0 + MXU_K, :])
            continue
        first = jnp.where(never, anchors[i].astype(w_ref.dtype), w_ref[r0:r0 + ar, 0:ac])
        top = jnp.concatenate([first, w_ref[r0:r0 + ar, ac:]], axis=1)
        pieces.append(jnp.concatenate([top, w_ref[r0 + ar:r0 + MXU_K, :]], axis=0))
    return jnp.concatenate(pieces, axis=0)


def _fold_to_anchor(x):
    ar, ac = ANCHOR_TILE
    tiles = [x[r:r + ar, c:c + ac] for r in range(0, x.shape[0], ar) for c in range(0, x.shape[1], ac)]
    out = tiles[0]
    for t in tiles[1:]:
        out = out + t
    return out


def _branch_b_prompt_kernel(h_ref, wua_ref, wug_ref, wzb_ref, cw_ref, cb_ref, lng_ref, lnb_ref,
                            yb_ref, ulast_ref, ua_ref, ug_ref, zb_ref, ush_ref, carry_ref, conv_ref, sz_ref,
                            *, tiles_per_seq, n_chunks, n_steps):
    s = pl.program_id(0)
    prev = jnp.maximum(s - 1, 0)
    mp, jp = prev // n_chunks, prev % n_chunks
    tm = h_ref.shape[0]
    cw = cw_ref.shape[1]
    n_rows = tm + HIST_B
    lanes_p = pl.ds(pl.multiple_of(jp * cw, cw), cw)

    @pl.when(s == 0)
    def _():
        for ref in (ua_ref, ug_ref, zb_ref, carry_ref):
            ref[...] = jnp.zeros(ref.shape, F32)

    u = ua_ref[...] * jax.nn.sigmoid(ug_ref[...])
    szv = _silu(zb_ref[...])
    sz_ref[:, lanes_p] = szv
    hist = jnp.where(mp % tiles_per_seq != 0, carry_ref[jp], 0.0)
    tail = u[tm - HIST_B:, :]
    carry_ref[jp] = tail
    ulast_ref[0] = tail
    staged = jnp.concatenate([hist, u], axis=0)
    ush_ref[0] = staged
    for r in range(1, SUBLANES):
        ush_ref[r] = pltpu.roll(staged, n_rows - r, axis=0)

    first = HIST_B - (CONV_B - 1)

    def conv_rows(r0):
        acc = cb_ref[...]
        for k in range(CONV_B):
            r, q = (first + k) % SUBLANES, (first + k) // SUBLANES
            acc = acc + cw_ref[k:k + 1, :] * ush_ref[r, r0 + q * SUBLANES:r0 + q * SUBLANES + CONV_ROWS, :]
        conv_ref[r0:r0 + CONV_ROWS, lanes_p] = acc
        return _fold_to_anchor(acc)

    deps = [conv_rows(r0) for r0 in range(0, tm, CONV_ROWS)]
    deps.append(_fold_to_anchor(szv[:, 0:ANCHOR_TILE[1]]))

    n_slabs = h_ref.shape[1] // MXU_K
    first_slab, last_slab = 6, 3 * n_slabs - 2
    anchors = [{}, {}, {}]
    for i, dep in enumerate(deps):
        slab = first_slab + (last_slab - first_slab) * i // (len(deps) - 1)
        anchors[slab // n_slabs][slab % n_slabs] = dep
    never = s < 0
    h = h_ref[...]
    ua_ref[...] = jnp.dot(h, _anchored_weights(wua_ref, anchors[0], never), preferred_element_type=F32)
    ug_ref[...] = jnp.dot(h, _anchored_weights(wug_ref, anchors[1], never), preferred_element_type=F32)
    zb_ref[...] = jnp.dot(h, _anchored_weights(wzb_ref, anchors[2], never), preferred_element_type=F32)

    @pl.when(jnp.logical_and(s > 0, jp == n_chunks - 1))
    def _():
        _layernorm_gate(conv_ref, sz_ref, lng_ref, lnb_ref, yb_ref)


def _branch_b_prompt(h, w_in, conv_w, conv_b, ln_g, ln_b, w_b, col0, *, tm, cw, seq_len):
    m, d = h.shape
    n_chunks = w_b // cw
    n_tiles = m // tm
    n_steps = n_tiles * n_chunks
    tiles_per_seq = seq_len // tm
    cur = lambda s: jnp.minimum(s, n_steps - 1)
    prev = lambda s: jnp.maximum(s - 1, 0)
    col = lambda k: (lambda s: (0, col0 // cw + k * n_chunks + cur(s) % n_chunks))
    in_specs = [pl.BlockSpec((tm, d), lambda s: (cur(s) // n_chunks, 0))]
    in_specs += [pl.BlockSpec((d, cw), col(k)) for k in (0, 1, 2)]
    in_specs += [pl.BlockSpec((CONV_B, cw), lambda s: (0, prev(s) % n_chunks)),
                 pl.BlockSpec((1, cw), lambda s: (0, prev(s) % n_chunks)),
                 pl.BlockSpec((1, w_b), lambda s: (0, 0)),
                 pl.BlockSpec((1, w_b), lambda s: (0, 0))]
    kernel = functools.partial(_branch_b_prompt_kernel, tiles_per_seq=tiles_per_seq,
                               n_chunks=n_chunks, n_steps=n_steps)
    yb, tail = pl.pallas_call(
        kernel, grid=(n_steps + 1,), in_specs=in_specs,
        out_specs=[pl.BlockSpec((tm, w_b), lambda s: (prev(s) // n_chunks, 0)),
                   pl.BlockSpec((1, HIST_B, cw), lambda s: (prev(s) // n_chunks, 0, prev(s) % n_chunks))],
        out_shape=[jax.ShapeDtypeStruct((m, w_b), BF16),
                   jax.ShapeDtypeStruct((n_tiles, HIST_B, w_b), F32)],
        scratch_shapes=[pltpu.VMEM((tm, cw), F32),
                        pltpu.VMEM((tm, cw), F32),
                        pltpu.VMEM((tm, cw), F32),
                        pltpu.VMEM((SUBLANES, tm + HIST_B, cw), F32),
                        pltpu.VMEM((n_chunks, HIST_B, cw), F32),
                        pltpu.VMEM((tm, w_b), F32),
                        pltpu.VMEM((tm, w_b), F32)],
        compiler_params=_params(1),
    )(h, w_in, w_in, w_in, conv_w, conv_b.reshape(1, w_b), ln_g.reshape(1, w_b), ln_b.reshape(1, w_b))
    return yb, tail[tiles_per_seq - 1::tiles_per_seq]


def _branch_b_sample_kernel(h_ref, wua_ref, wug_ref, wzb_ref, cw_ref, cb_ref, lng_ref, lnb_ref, st_ref,
                            yb_ref, u_ref, conv_ref, sz_ref):
    j = pl.program_id(1)
    cw = cw_ref.shape[1]
    h = h_ref[...]
    u = _dot(h, wua_ref) * jax.nn.sigmoid(_dot(h, wug_ref))
    u_ref[...] = u
    lanes = pl.ds(pl.multiple_of(j * cw, cw), cw)
    sz_ref[:, lanes] = _silu(_dot(h, wzb_ref))
    acc = cb_ref[...] + cw_ref[CONV_B - 1:CONV_B, :] * u
    for k in range(CONV_B - 1):
        acc = acc + cw_ref[k:k + 1, :] * st_ref[k]
    conv_ref[:, lanes] = acc

    @pl.when(j == pl.num_programs(1) - 1)
    def _():
        _layernorm_gate(conv_ref, sz_ref, lng_ref, lnb_ref, yb_ref)


def _branch_b_sample(h, w_in, conv_w, conv_b, ln_g, ln_b, w_b, col0, *, tm, cw, state):
    m, d = h.shape
    n_chunks = w_b // cw
    col = lambda k: (lambda i, j: (0, col0 // cw + k * n_chunks + j))
    in_specs = [pl.BlockSpec((tm, d), lambda i, j: (i, 0))]
    in_specs += [pl.BlockSpec((d, cw), col(k)) for k in (0, 1, 2)]
    in_specs += [pl.BlockSpec((CONV_B, cw), lambda i, j: (0, j)),
                 pl.BlockSpec((1, cw), lambda i, j: (0, j)),
                 pl.BlockSpec((1, w_b), lambda i, j: (0, 0)),
                 pl.BlockSpec((1, w_b), lambda i, j: (0, 0)),
                 pl.BlockSpec((CONV_B - 1, tm, cw), lambda i, j: (0, i, j))]
    return pl.pallas_call(
        _branch_b_sample_kernel, grid=(m // tm, n_chunks), in_specs=in_specs,
        out_specs=[pl.BlockSpec((tm, w_b), lambda i, j: (i, 0)),
                   pl.BlockSpec((tm, cw), lambda i, j: (i, j))],
        out_shape=[jax.ShapeDtypeStruct((m, w_b), BF16), jax.ShapeDtypeStruct((m, w_b), F32)],
        scratch_shapes=[pltpu.VMEM((tm, w_b), F32), pltpu.VMEM((tm, w_b), F32)],
        compiler_params=_params(2),
    )(h, w_in, w_in, w_in, conv_w, conv_b.reshape(1, w_b), ln_g.reshape(1, w_b), ln_b.reshape(1, w_b),
      jnp.transpose(state, (1, 0, 2)))


def _merge_kernel(h_ref, ya_ref, yb_ref, hx_ref, yax_ref, ybx_ref, wga_ref, wgb_ref, woa_ref, wob_ref,
                  mix_ref, mixx_ref):
    m, last = pl.program_id(0), pl.num_programs(0) - 1
    tm = h_ref.shape[0]

    def mix_of(h, ya, yb):
        return (jax.nn.sigmoid(_dot(h, wga_ref)) * _dot(ya, woa_ref)
                + jax.nn.sigmoid(_dot(h, wgb_ref)) * _dot(yb, wob_ref))

    @pl.when(m != last)
    def _():
        mix_ref[...] = mix_of(h_ref[...], ya_ref[...], yb_ref[...]).astype(mix_ref.dtype)

    @pl.when(m == last)
    def _():
        rows = lambda a, b: jnp.concatenate([a[...], b[...]], axis=0)
        mix = mix_of(rows(h_ref, hx_ref), rows(ya_ref, yax_ref), rows(yb_ref, ybx_ref))
        mix_ref[...] = mix[:tm].astype(mix_ref.dtype)
        mixx_ref[...] = mix[tm:].astype(mixx_ref.dtype)


def _merge(h, ya, yb, extra, w_in, w_out_a, w_out_b, col0, *, tm, cn):
    m, d = h.shape
    mx = extra[0].shape[0]
    w = ya.shape[1]
    n_chunks = d // cn
    n_tiles = m // tm
    gcol = lambda k: (lambda i, j: (0, col0 // cn + k * n_chunks + j))
    whole = lambda i, j: (0, 0)
    return pl.pallas_call(
        _merge_kernel,
        grid=(n_tiles, n_chunks),
        in_specs=[pl.BlockSpec((tm, d), lambda i, j: (i, 0)),
                  pl.BlockSpec((tm, w), lambda i, j: (i, 0)),
                  pl.BlockSpec((tm, w), lambda i, j: (i, 0)),
                  pl.BlockSpec((mx, d), whole),
                  pl.BlockSpec((mx, w), whole),
                  pl.BlockSpec((mx, w), whole),
                  pl.BlockSpec((d, cn), gcol(0)),
                  pl.BlockSpec((d, cn), gcol(1)),
                  pl.BlockSpec((w, cn), lambda i, j: (0, j)),
                  pl.BlockSpec((w, cn), lambda i, j: (0, j))],
        out_specs=[pl.BlockSpec((tm, cn), lambda i, j: (i, j)),
                   pl.BlockSpec((mx, cn), lambda i, j: (0, jnp.where(i == n_tiles - 1, j, 0)))],
        out_shape=[jax.ShapeDtypeStruct((m, d), BF16), jax.ShapeDtypeStruct((mx, d), BF16)],
        compiler_params=_params(2),
    )(h, ya, yb, *extra, w_in, w_in, w_out_a, w_out_b)


def _out_kernel(mix_ref, x_ref, wo_ref, g_ref, y_ref, ssq_ref):
    j = pl.program_id(1)
    cn = wo_ref.shape[1]
    o = x_ref[...] + _dot(mix_ref[...], wo_ref)
    y_ref[:, pl.ds(pl.multiple_of(j * cn, cn), cn)] = o
    part = jnp.sum(o * o, axis=-1, keepdims=True)

    @pl.when(j == 0)
    def _():
        ssq_ref[...] = part

    @pl.when(j != 0)
    def _():
        ssq_ref[...] += part

    @pl.when(j == pl.num_programs(1) - 1)
    def _():
        scale = lax.rsqrt(ssq_ref[...] / y_ref.shape[1] + EPS)
        y_ref[...] = y_ref[...] * scale * g_ref[...]


def _out_proj(mix, x, w_o, gain, *, tm, cn):
    m, d = x.shape
    return pl.pallas_call(
        _out_kernel,
        grid=(m // tm, d // cn),
        in_specs=[pl.BlockSpec((tm, d), lambda i, j: (i, 0)),
                  pl.BlockSpec((tm, cn), lambda i, j: (i, j)),
                  pl.BlockSpec((d, cn), lambda i, j: (0, j)),
                  pl.BlockSpec((1, d), lambda i, j: (0, 0))],
        out_specs=pl.BlockSpec((tm, d), lambda i, j: (i, 0)),
        out_shape=jax.ShapeDtypeStruct((m, d), F32),
        scratch_shapes=[pltpu.VMEM((tm, 1), F32)],
        compiler_params=_params(2),
    )(mix, x, w_o, gain.reshape(1, d))


def _tiles(m):
    if m >= 1024:
        return dict(norm=512, a=1024, b=512, merge=512, out=512, a_cw=256, b_cw=256, merge_cn=512)
    return dict(norm=m, a=m, b=m, merge=m, out=m, a_cw=256, b_cw=256, merge_cn=512)


def _branches(x, w, bf16_w=None, *, seq_len=None, state_a=None, state_b=None):
    m, d = x.shape
    w_a = w["conv_a_w"].shape[1]
    w_b = w["conv_b_w"].shape[1]
    t = _tiles(m)
    h = _rmsnorm_cast(x, w["norm_gain"], t["norm"])
    casts = ()
    if bf16_w is None:
        w_in = w["w_in"]
        casts = ((w_in, 4 * w_a, w_in.shape[1] - 4 * w_a), (w["w_out_a"], 0, d), (w["w_out_b"], 0, d),
                 (w["w_o"], 0, d))
    ya, a_tail, cast_out = _branch_a(h, w["w_in_a"], w["conv_a_w"], w["conv_a_b"], w_a,
                                     tm=t["a"], cw=t["a_cw"], seq_len=seq_len, state=state_a, casts=casts)
    if bf16_w is None:
        bf16_w = tuple(cast_out)
    w_rest, w_out_a, w_out_b, w_o = bf16_w
    b_args = (h, w_rest, w["conv_b_w"], w["conv_b_b"], w["ln_b_gain"], w["ln_b_bias"], w_b, 0)
    if state_b is None:
        yb, b_tail = _branch_b_prompt(*b_args, tm=t["b"], cw=t["b_cw"], seq_len=seq_len)
    else:
        yb, b_tail = _branch_b_sample(*b_args, tm=t["b"], cw=t["b_cw"], state=state_b)
    return (h, ya, yb), a_tail, b_tail, bf16_w


def kernel(x_prompt, x_sample, state_conv_a, state_conv_b, norm_gain, w_in, conv_a_w, conv_a_b, w_out_a,
           conv_b_w, conv_b_b, ln_b_gain, ln_b_bias, w_out_b, w_o, final_norm_gain):
    depth = norm_gain.shape[0]
    assert depth == 1, "the output projection kernel applies the final norm: one layer only"
    batch, seq, d = x_prompt.shape
    dec_batch, dec_seq, _ = x_sample.shape
    assert dec_seq == 1
    w_a = conv_a_w.shape[2]
    w = dict(norm_gain=norm_gain[0], w_in=w_in[0], w_in_a=w_in[0, :, :4 * w_a].astype(BF16),
             conv_a_w=conv_a_w[0], conv_a_b=conv_a_b[0], w_out_a=w_out_a[0], conv_b_w=conv_b_w[0],
             conv_b_b=conv_b_b[0], ln_b_gain=ln_b_gain[0], ln_b_bias=ln_b_bias[0], w_out_b=w_out_b[0],
             w_o=w_o[0], final_norm_gain=final_norm_gain)

    xp, xs = x_prompt.reshape(batch * seq, d), x_sample.reshape(dec_batch, d)
    acts_p, a_tail, b_tail, bf16_w = _branches(xp, w, seq_len=seq)
    new_a_prompt = a_tail[:, HIST_A - (CONV_A - 1):, :]
    new_b_prompt = b_tail[:, HIST_B - (CONV_B - 1):, :]

    sa, sb = state_conv_a[0], state_conv_b[0]
    acts_s, cv_s, u_s, _ = _branches(xs, w, bf16_w, state_a=sa, state_b=sb)
    new_a_sample = jnp.concatenate([sa[:, 1:, :], cv_s[:, None, :]], axis=1)
    new_b_sample = jnp.concatenate([sb[:, 1:, :], u_s[:, None, :]], axis=1)

    w_rest, w_out_a_bf, w_out_b_bf, w_o_bf = bf16_w
    tp, ts = _tiles(xp.shape[0]), _tiles(xs.shape[0])
    mix_p, mix_s = _merge(*acts_p, acts_s, w_rest, w_out_a_bf, w_out_b_bf, 3 * conv_b_w.shape[2],
                          tm=tp["merge"], cn=tp["merge_cn"])
    yp = _out_proj(mix_p, xp, w_o_bf, final_norm_gain, tm=tp["out"], cn=1024)
    ys = _out_proj(mix_s, xs, w_o_bf, final_norm_gain, tm=ts["out"], cn=1024)

    return (yp.reshape(batch, seq, d), ys.reshape(dec_batch, dec_seq, d),
            new_a_prompt[None], new_b_prompt[None], new_a_sample[None], new_b_sample[None])
```
